```python
import math
import jax, jax.numpy as jnp
from jax import lax
import numpy as np

D_MODEL = 1024
BATCH = 8
SEQ = 2048
DEPTH = 4

MEM_LEN = 256
EPS = 1e-6
SSM_HEADS = 16
SSM_HEAD_DIM = 64
D_SSM = SSM_HEADS * SSM_HEAD_DIM
SSM_GROUPS = 4
SSM_STATE = 128
SSM_CONV = 4
SSM_CHUNK = 128
CONV_CH = D_SSM + 2 * SSM_GROUPS * SSM_STATE
MLA_HEADS = 16
QK_NOPE = 64
QK_ROPE = 32
V_DIM = 64
Q_LORA = 384
KV_LORA = 256
D_ATTN = MLA_HEADS * V_DIM
ROPE_THETA = 10000.0
Q_BLOCK = 128
D_MIX = D_SSM + D_ATTN
_O1 = D_SSM
_O2 = _O1 + CONV_CH
_O3 = _O2 + SSM_HEADS
_O4 = _O3 + Q_LORA
_O5 = _O4 + KV_LORA
D_IN = _O5 + QK_ROPE
IN_SPLITS = (_O1, _O2, _O3, _O4, _O5)
MEM_HEADS = 4
MEM_HEAD_DIM = D_MODEL // MEM_HEADS
D_FF = 2816
FFN_CONV = 3

kernel_name = "hymba_ssd_mla_memxattn_convffn"


def rmsnorm(x, g):
    xf = x.astype(jnp.float32)
    var = jnp.mean(xf * xf, axis=-1, keepdims=True)
    return (xf * lax.rsqrt(var + EPS) * g.astype(jnp.float32)).astype(x.dtype)


def causal_dwconv(x, w, b):
    k = w.shape[0]
    s = x.shape[1]
    xp = jnp.pad(x, ((0, 0), (k - 1, 0), (0, 0)))
    y = xp[:, 0:s] * w[0]
    for j in range(1, k):
        y = y + xp[:, j:j + s] * w[j]
    return y + b


def rope_tables(positions):
    inv_freq = 1.0 / (ROPE_THETA ** (jnp.arange(0, QK_ROPE, 2, dtype=jnp.float32) / QK_ROPE))
    ang = positions.astype(jnp.float32)[..., None] * inv_freq
    return jnp.cos(ang), jnp.sin(ang)


def apply_rope(t, cos, sin):
    half = t.shape[-1] // 2
    t1, t2 = t[..., :half], t[..., half:]
    out = jnp.concatenate([t1 * cos - t2 * sin, t2 * cos + t1 * sin], axis=-1)
    return out.astype(t.dtype)


def segsum_exp(a):
    t = a.shape[-1]
    cs = jnp.cumsum(a, axis=-1)
    diff = cs[..., :, None] - cs[..., None, :]
    mask = jnp.tril(jnp.ones((t, t), dtype=bool))
    return jnp.exp(jnp.where(mask, diff, -jnp.inf))


def ssd_scan(x, dt, a, bm, cm):
    out_dtype = x.dtype
    x = x.astype(jnp.float32)
    bm = bm.astype(jnp.float32)
    cm = cm.astype(jnp.float32)
    b, l, h, p = x.shape
    g, n = bm.shape[-2:]
    r = h // g
    c = l // SSM_CHUNK
    xd = (x * dt[..., None]).reshape(b, c, SSM_CHUNK, g, r, p)
    ad = jnp.moveaxis((dt * a).reshape(b, c, SSM_CHUNK, g, r), 2, -1)
    a_cs = jnp.cumsum(ad, axis=-1)
    bc = bm.reshape(b, c, SSM_CHUNK, g, n)
    cc = cm.reshape(b, c, SSM_CHUNK, g, n)
    lmat = segsum_exp(ad)
    cb = jnp.einsum('bclgn,bcsgn->bcgls', cc, bc)
    y_diag = jnp.einsum('bcgls,bcgrls,bcsgrp->bclgrp', cb, lmat, xd)
    decay_states = jnp.exp(a_cs[..., -1:] - a_cs)
    states = jnp.einsum('bclgn,bcgrl,bclgrp->bcgrpn', bc, decay_states, xd)
    chunk_decay = jnp.exp(a_cs[..., -1])

    def step(prev, inp):
        st, dec = inp
        return prev * dec[..., None, None] + st, prev

    init = jnp.zeros((b, g, r, p, n), jnp.float32)
    _, prev_states = lax.scan(step, init, (jnp.moveaxis(states, 1, 0), jnp.moveaxis(chunk_decay, 1, 0)))
    prev_states = jnp.moveaxis(prev_states, 0, 1)
    y_off = jnp.einsum('bclgn,bcgrpn,bcgrl->bclgrp', cc, prev_states, jnp.exp(a_cs))
    return (y_diag + y_off).reshape(b, l, h, p).astype(out_dtype)


def mla_attention(c_q, c_kv, k_rope, q_norm, w_uq, kv_norm, w_ukv, cos, sin):
    b, s, _ = c_q.shape
    q = (rmsnorm(c_q, q_norm) @ w_uq).reshape(b, s, MLA_HEADS, QK_NOPE + QK_ROPE)
    q_nope = q[..., :QK_NOPE]
    q_pe = apply_rope(q[..., QK_NOPE:], cos[:, :, None, :], sin[:, :, None, :])
    kv = (rmsnorm(c_kv, kv_norm) @ w_ukv).reshape(b, s, MLA_HEADS, QK_NOPE + V_DIM)
    k_nope, v = kv[..., :QK_NOPE], kv[..., QK_NOPE:]
    k_pe = apply_rope(k_rope, cos, sin)
    scale = (QK_NOPE + QK_ROPE) ** -0.5
    outs = []
    for i in range(s // Q_BLOCK):
        q0 = i * Q_BLOCK
        kend = q0 + Q_BLOCK
        sc = (jnp.einsum('bqhd,bkhd->bhqk', q_nope[:, q0:kend], k_nope[:, :kend])
              + jnp.einsum('bqhr,bkr->bhqk', q_pe[:, q0:kend], k_pe[:, :kend]))
        sc = sc.astype(jnp.float32) * scale
        mask = (q0 + jnp.arange(Q_BLOCK))[:, None] >= jnp.arange(kend)[None, :]
        pr = jax.nn.softmax(jnp.where(mask, sc, -jnp.inf), axis=-1).astype(v.dtype)
        outs.append(jnp.einsum('bhqk,bkhd->bqhd', pr, v[:, :kend]))
    return jnp.concatenate(outs, axis=1).reshape(b, s, D_ATTN)


def memory_attention(h, m, w_q, w_k, w_v, w_o):
    b, s, _ = h.shape
    ml = m.shape[1]
    q = (h @ w_q).reshape(b, s, MEM_HEADS, MEM_HEAD_DIM)
    k = (m @ w_k).reshape(b, ml, MEM_HEADS, MEM_HEAD_DIM)
    v = (m @ w_v).reshape(b, ml, MEM_HEADS, MEM_HEAD_DIM)
    sc = jnp.einsum('bqhd,bkhd->bhqk', q, k).astype(jnp.float32) * (MEM_HEAD_DIM ** -0.5)
    pr = jax.nn.softmax(sc, axis=-1).astype(v.dtype)
    o = jnp.einsum('bhqk,bkhd->bqhd', pr, v).reshape(b, s, D_MODEL)
    return o @ w_o


def conv_glu_ffn(h, w_up, conv_w, conv_b, w_down):
    u = causal_dwconv(h @ w_up, conv_w, conv_b)
    gate, val = u[..., :D_FF], u[..., D_FF:]
    return (jax.nn.silu(gate) * val) @ w_down


def setup_inputs(seed: int = 0) -> dict:
    key = jax.random.key(seed)
    ks = iter(jax.random.split(key, 64))

    def nrm(shape, scale):
        return jax.random.normal(next(ks), shape, jnp.float32) * scale

    def gain(shape):
        return 1.0 + nrm(shape, 0.02)

    L = DEPTH
    dt0 = jnp.exp(jax.random.uniform(next(ks), (L, SSM_HEADS), jnp.float32,
                                     math.log(1e-3), math.log(1e-1)))
    dt_bias = dt0 + jnp.log(-jnp.expm1(-dt0))
    a_log = jnp.log(jax.random.uniform(next(ks), (L, SSM_HEADS), jnp.float32, 1.0, 16.0))
    offsets = jax.random.randint(next(ks), (BATCH, 1), 0, 1024, dtype=jnp.int32)
    positions = offsets + jnp.arange(SEQ, dtype=jnp.int32)[None, :]
    return {
        "x": nrm((BATCH, SEQ, D_MODEL), 1.0),
        "mem": nrm((BATCH, MEM_LEN, D_MODEL), 1.0),
        "positions": positions,
        "norm_mix": gain((L, D_MODEL)),
        "w_in": nrm((L, D_MODEL, D_IN), D_MODEL ** -0.5),
        "ssm_conv_w": nrm((L, SSM_CONV, CONV_CH), SSM_CONV ** -0.5),
        "ssm_conv_b": nrm((L, CONV_CH), 0.02),
        "dt_bias": dt_bias,
        "a_log": a_log,
        "d_skip": 1.0 + nrm((L, SSM_HEADS), 0.1),
        "ssm_norm": gain((L, D_SSM)),
        "q_norm": gain((L, Q_LORA)),
        "w_uq": nrm((L, Q_LORA, MLA_HEADS * (QK_NOPE + QK_ROPE)), Q_LORA ** -0.5),
        "kv_norm": gain((L, KV_LORA)),
        "w_ukv": nrm((L, KV_LORA, MLA_HEADS * (QK_NOPE + V_DIM)), KV_LORA ** -0.5),
        "attn_out_norm": gain((L, D_ATTN)),
        "w_out": nrm((L, D_MIX, D_MODEL), D_MIX ** -0.5),
        "norm_mem_q": gain((L, D_MODEL)),
        "norm_mem_kv": gain((L, D_MODEL)),
        "w_mq": nrm((L, D_MODEL, D_MODEL), D_MODEL ** -0.5),
        "w_mk": nrm((L, D_MODEL, D_MODEL), D_MODEL ** -0.5),
        "w_mv": nrm((L, D_MODEL, D_MODEL), D_MODEL ** -0.5),
        "w_mo": nrm((L, D_MODEL, D_MODEL), D_MODEL ** -0.5),
        "norm_ffn": gain((L, D_MODEL)),
        "w_up": nrm((L, D_MODEL, 2 * D_FF), D_MODEL ** -0.5),
        "ffn_conv_w": nrm((L, FFN_CONV, 2 * D_FF), FFN_CONV ** -0.5),
        "ffn_conv_b": nrm((L, 2 * D_FF), 0.02),
        "w_down": nrm((L, D_FF, D_MODEL), D_FF ** -0.5),
        "final_norm": gain((D_MODEL,)),
    }


def reference(x, mem, positions, norm_mix, w_in, ssm_conv_w, ssm_conv_b, dt_bias, a_log,
              d_skip, ssm_norm, q_norm, w_uq, kv_norm, w_ukv, attn_out_norm, w_out,
              norm_mem_q, norm_mem_kv, w_mq, w_mk, w_mv, w_mo, norm_ffn, w_up,
              ffn_conv_w, ffn_conv_b, w_down, final_norm):
    b, s, _ = x.shape
    cos, sin = rope_tables(positions)
    for i in range(DEPTH):
        h = rmsnorm(x, norm_mix[i])
        proj = h @ w_in[i]
        z, xbc, dt_raw, c_q, c_kv, k_rope = jnp.split(proj, IN_SPLITS, axis=-1)
        xbc = jax.nn.silu(causal_dwconv(xbc, ssm_conv_w[i], ssm_conv_b[i]))
        xs = xbc[..., :D_SSM].reshape(b, s, SSM_HEADS, SSM_HEAD_DIM)
        bm = xbc[..., D_SSM:D_SSM + SSM_GROUPS * SSM_STATE].reshape(b, s, SSM_GROUPS, SSM_STATE)
        cm = xbc[..., D_SSM + SSM_GROUPS * SSM_STATE:].reshape(b, s, SSM_GROUPS, SSM_STATE)
        dt = jax.nn.softplus(dt_raw.astype(jnp.float32) + dt_bias[i].astype(jnp.float32))
        a = -jnp.exp(a_log[i].astype(jnp.float32))
        y = ssd_scan(xs, dt, a, bm, cm) + xs * d_skip[i][:, None]
        y_ssm = rmsnorm(y.reshape(b, s, D_SSM) * jax.nn.silu(z), ssm_norm[i])
        y_att = mla_attention(c_q, c_kv, k_rope, q_norm[i], w_uq[i], kv_norm[i], w_ukv[i], cos, sin)
        y_att = rmsnorm(y_att, attn_out_norm[i])
        x = x + jnp.concatenate([y_ssm, y_att], axis=-1) @ w_out[i]
        x = x + memory_attention(rmsnorm(x, norm_mem_q[i]), rmsnorm(mem, norm_mem_kv[i]),
                                 w_mq[i], w_mk[i], w_mv[i], w_mo[i])
        x = x + conv_glu_ffn(rmsnorm(x, norm_ffn[i]), w_up[i], ffn_conv_w[i], ffn_conv_b[i], w_down[i])
    return rmsnorm(x, final_norm)
```

```python
import functools
import math

import jax
import jax.numpy as jnp
from jax import lax
from jax.experimental import pallas as pl
from jax.experimental.pallas import tpu as pltpu

F32 = jnp.float32
BF16 = jnp.bfloat16

D_MODEL = 1024
EPS = 1e-6
SSM_HEADS = 16
SSM_HEAD_DIM = 64
D_SSM = SSM_HEADS * SSM_HEAD_DIM
SSM_GROUPS = 4
SSM_STATE = 128
SSM_CONV = 4
SSM_CHUNK = 128
CONV_CH = D_SSM + 2 * SSM_GROUPS * SSM_STATE
MLA_HEADS = 16
QK_NOPE = 64
QK_ROPE = 32
V_DIM = 64
Q_LORA = 384
KV_LORA = 256
D_ATTN = MLA_HEADS * V_DIM
ROPE_THETA = 10000.0
MEM_HEADS = 4
MEM_HEAD_DIM = D_MODEL // MEM_HEADS
D_FF = 2816
FFN_CONV = 3

LANES = 128
SUBLANES = 8
VMEM_LIMIT = 56 * 1024 * 1024

ROPE_LO = QK_NOPE
ROPE_MID = QK_NOPE + QK_ROPE // 2
ROPE_HI = QK_NOPE + QK_ROPE
DT_LO = ROPE_HI
DT_HI = ROPE_HI + SSM_HEADS

FFN_CHUNK = 256
ATTN_TQ = 256
ATTN_TK = 256


def _dot(a, b):
    return jnp.dot(a, b, preferred_element_type=F32)


def _dot_nt(a, b):
    return lax.dot_general(a, b, (((1,), (1,)), ((), ())), preferred_element_type=F32)


def _rms(xf, g):
    var = jnp.mean(xf * xf, axis=-1, keepdims=True)
    return xf * lax.rsqrt(var + EPS) * g


def _silu(x):
    return x * (1.0 / (1.0 + jnp.exp(-x)))


def _softplus(x):
    return jnp.maximum(x, 0.0) + jnp.log1p(jnp.exp(-jnp.abs(x)))


def _rope128(t, c, s, lane):
    rot = jnp.where(lane < ROPE_MID, pltpu.roll(t, LANES - QK_ROPE // 2, 1),
                    pltpu.roll(t, QK_ROPE // 2, 1))
    return t * c + rot * s


def _params(sem):
    return pltpu.CompilerParams(dimension_semantics=sem, vmem_limit_bytes=VMEM_LIMIT)


def _const_spec(shape, index):
    return pl.BlockSpec(shape, lambda *_: index, pipeline_mode=pl.Buffered(1))


def _rope_kernel(pos_ref, inv_ref, cos_ref, sin_ref):
    ang = pos_ref[...] * inv_ref[...]
    cos_ref[...] = jnp.cos(ang)
    sin_ref[...] = jnp.sin(ang)


def _rope_tables(positions):
    t = positions.size
    half = QK_ROPE // 2
    per_row = LANES // half
    inv_freq = 1.0 / (ROPE_THETA ** (jnp.arange(0, QK_ROPE, 2, dtype=F32) / QK_ROPE))
    posd = jnp.repeat(positions.reshape(t).astype(F32), half).reshape(t // per_row, LANES)
    invd = jnp.tile(inv_freq, per_row).reshape(1, LANES)
    rows = t // per_row
    cosd, sind = pl.pallas_call(
        _rope_kernel,
        out_shape=(jax.ShapeDtypeStruct((rows, LANES), F32),) * 2,
        name="rope_tables",
    )(posd, invd)
    cos = cosd.reshape(t, half)
    sin = sind.reshape(t, half)
    ctab = jnp.concatenate([jnp.ones((t, QK_NOPE), F32), cos, cos,
                            jnp.zeros((t, LANES - ROPE_HI), F32)], axis=-1)
    stab = jnp.concatenate([jnp.zeros((t, QK_NOPE), F32), -sin, sin,
                            jnp.zeros((t, LANES - ROPE_HI), F32)], axis=-1)
    return ctab, stab


_O_Z = 0
_O_XBC = D_SSM
_O_CQ = _O_XBC + CONV_CH
_O_CKV = _O_CQ + Q_LORA
_W_MAIN = _O_CKV + KV_LORA
_QK_SCALE = (QK_NOPE + QK_ROPE) ** -0.5


def _mix_in_kernel(x_ref, g_ref, wm_ref, ws_ref, qg_ref, wuq_ref, kg_ref, wuk_ref, wuv_ref,
                   c_ref, s_ref, z_ref, xbc_ref, sm_ref, q_ref, k_ref, v_ref):
    h = _rms(x_ref[...], g_ref[...]).astype(BF16)
    z_ref[...] = _dot(h, wm_ref[:, _O_Z:_O_XBC]).astype(BF16)
    xbc_ref[...] = _dot(h, wm_ref[:, _O_XBC:_O_CQ]).astype(BF16)
    cq = _dot(h, wm_ref[:, _O_CQ:_O_CKV])
    ckv = _dot(h, wm_ref[:, _O_CKV:_W_MAIN])
    sm = _dot(h, ws_ref[...])
    sm_ref[...] = sm

    c = c_ref[...]
    s = s_ref[...]
    lane = lax.broadcasted_iota(jnp.int32, c.shape, 1)
    cqn = _rms(cq, qg_ref[...]).astype(BF16)
    qq = _dot(cqn, wuq_ref[...])
    cs = c * _QK_SCALE
    ss = s * _QK_SCALE
    for hd in range(MLA_HEADS):
        sl = slice(hd * LANES, (hd + 1) * LANES)
        q_ref[:, sl] = _rope128(qq[:, sl], cs, ss, lane).astype(BF16)

    kpe = _rope128(sm, c, s, lane)
    kpe = jnp.where((lane >= ROPE_LO) & (lane < ROPE_HI), kpe, 0.0)
    ckvn = _rms(ckv, kg_ref[...]).astype(BF16)
    kk = _dot(ckvn, wuk_ref[...])
    for hd in range(MLA_HEADS):
        sl = slice(hd * LANES, (hd + 1) * LANES)
        k_ref[:, sl] = (kk[:, sl] + kpe).astype(BF16)
    v_ref[...] = _dot(ckvn, wuv_ref[...]).astype(BF16)


def _mix_in(x2d, l, p, ctab, stab, tm):
    t = x2d.shape[0]
    hp = MLA_HEADS * LANES
    row = lambda w: pl.BlockSpec((tm, w), lambda i: (i, 0))
    lay = lambda shape: _const_spec((None,) + shape, (l, 0, 0))
    return pl.pallas_call(
        _mix_in_kernel,
        grid=(t // tm,),
        in_specs=[row(D_MODEL), lay((1, D_MODEL)), lay((D_MODEL, _W_MAIN)), lay((D_MODEL, LANES)),
                  lay((1, Q_LORA)), lay((Q_LORA, hp)), lay((1, KV_LORA)), lay((KV_LORA, hp)),
                  lay((KV_LORA, D_ATTN)), row(LANES), row(LANES)],
        out_specs=[row(D_SSM), row(CONV_CH), row(LANES), row(hp), row(hp), row(D_ATTN)],
        out_shape=[jax.ShapeDtypeStruct((t, D_SSM), BF16), jax.ShapeDtypeStruct((t, CONV_CH), BF16),
                   jax.ShapeDtypeStruct((t, LANES), F32), jax.ShapeDtypeStruct((t, hp), BF16),
                   jax.ShapeDtypeStruct((t, hp), BF16), jax.ShapeDtypeStruct((t, D_ATTN), BF16)],
        compiler_params=_params(("parallel",)),
        name="mix_in",
    )(x2d, p["norm_mix"], p["w_main"], p["w_small"], p["q_norm"], p["w_uq"], p["kv_norm"],
      p["w_uk"], p["w_uv"], ctab, stab)


def _split3(a):
    hi = a.astype(BF16)
    r = a - hi.astype(F32)
    mid = r.astype(BF16)
    lo = (r - mid.astype(F32)).astype(BF16)
    return hi, mid, lo


def _ssd_kernel(xbc_ref, halo_ref, z_ref, sm_ref, cw_ref, cb_ref, dtb_ref, alog_ref, dsk_ref,
                g_ref, o_ref, state_ref, *, tc):
    i = pl.program_id(1)

    @pl.when(i == 0)
    def _():
        state_ref[...] = jnp.zeros_like(state_ref)

    halo = jnp.where(i > 0, halo_ref[...].astype(F32), 0.0)
    full = jnp.concatenate([halo, xbc_ref[...].astype(F32)], axis=0)
    cw = cw_ref[...]
    acc = cb_ref[...] + cw[SSM_CONV - 1:SSM_CONV, :] * full[SUBLANES:, :]
    for j in range(SSM_CONV - 1):
        shift = SSM_CONV - 1 - j
        acc = acc + cw[j:j + 1, :] * pltpu.roll(full, shift, 0)[SUBLANES:, :]
    xc = _silu(acc)

    n = SSM_CHUNK
    ri = lax.broadcasted_iota(jnp.int32, (n, n), 0)
    ci = lax.broadcasted_iota(jnp.int32, (n, n), 1)
    causal = ri >= ci
    tri = jnp.where(causal, 1.0, 0.0).astype(BF16)
    lane = lax.broadcasted_iota(jnp.int32, (1, LANES), 1)
    dt_lanes = (lane >= DT_LO) & (lane < DT_HI)
    a_tile = jnp.where(dt_lanes, -jnp.exp(alog_ref[...]), 0.0)
    dtb = dtb_ref[...]
    bo = D_SSM
    co = D_SSM + SSM_GROUPS * SSM_STATE
    heads_per_group = SSM_HEADS // SSM_GROUPS

    for c in range(tc // n):
        rows = slice(c * n, (c + 1) * n)
        xs = xc[rows, 0:D_SSM]
        dtf = _softplus(sm_ref[rows, :] + dtb)
        ad = dtf * a_tile
        hi, mid, lo = _split3(ad)
        cs = _dot(tri, hi) + _dot(tri, mid) + _dot(tri, lo)
        cs_t = cs.T
        dt_t = dtf.T
        ys = []
        for g in range(SSM_GROUPS):
            bg = xc[rows, bo + g * SSM_STATE: bo + (g + 1) * SSM_STATE]
            cg = xc[rows, co + g * SSM_STATE: co + (g + 1) * SSM_STATE]
            bg_t = bg.T
            cg16 = cg.astype(BF16)
            cb = _dot_nt(cg16, bg.astype(BF16))
            for r in range(heads_per_group):
                hd = g * heads_per_group + r
                ln = DT_LO + hd
                col = cs[:, ln:ln + 1]
                row = cs_t[ln:ln + 1, :]
                dt_row = dt_t[ln:ln + 1, :]
                last = cs[n - 1:n, ln:ln + 1]
                lmat = jnp.exp(jnp.where(causal, col - row, -jnp.inf))
                mm = (cb * lmat * dt_row).astype(BF16)
                xh = xs[:, hd * SSM_HEAD_DIM:(hd + 1) * SSM_HEAD_DIM].astype(BF16)
                prev = state_ref[hd]
                y_h = _dot(mm, xh) + jnp.exp(col) * _dot(cg16, prev.astype(BF16))
                w_row = jnp.exp(last - row) * dt_row
                st_new = _dot((bg_t * w_row).astype(BF16), xh)
                state_ref[hd] = prev * jnp.exp(last) + st_new
                ys.append(y_h)
        y = jnp.concatenate(ys, axis=-1) + xs * dsk_ref[...]
        y = y * _silu(z_ref[rows, :].astype(F32))
        o_ref[rows, :] = _rms(y, g_ref[...]).astype(BF16)


def _ssd(xbc, z, sm, l, p, batch, seq, tc):
    ns = seq // tc
    halo_blocks_per_seq = seq // SUBLANES
    main = lambda w: pl.BlockSpec((tc, w), lambda b, i: (b * ns + i, 0))
    halo = pl.BlockSpec(
        (SUBLANES, CONV_CH),
        lambda b, i: (jnp.maximum(b * halo_blocks_per_seq + i * (tc // SUBLANES) - 1, 0), 0))
    lay = lambda shape: _const_spec((None,) + shape, (l, 0, 0))
    return pl.pallas_call(
        functools.partial(_ssd_kernel, tc=tc),
        grid=(batch, ns),
        in_specs=[main(CONV_CH), halo, main(D_SSM), main(LANES), lay((SSM_CONV, CONV_CH)),
                  lay((1, CONV_CH)), lay((1, LANES)), lay((1, LANES)), lay((1, D_SSM)),
                  lay((1, D_SSM))],
        out_specs=main(D_SSM),
        out_shape=jax.ShapeDtypeStruct((batch * seq, D_SSM), BF16),
        scratch_shapes=[pltpu.VMEM((SSM_HEADS, SSM_STATE, SSM_HEAD_DIM), F32)],
        compiler_params=_params(("arbitrary", "arbitrary")),
        name="ssd",
    )(xbc, xbc, z, sm, p["ssm_conv_w"], p["ssm_conv_b"], p["dt_bias"], p["a_log"], p["d_skip"],
      p["ssm_norm"])


def _attn_kernel(q_ref, k_ref, v_ref, o_ref, *, tq, tk):
    i = pl.program_id(2)
    ri = lax.broadcasted_iota(jnp.int32, (tq, tk), 0)
    ci = lax.broadcasted_iota(jnp.int32, (tq, tk), 1)
    diag_mask = ri >= ci
    outs = []
    for hh in range(2):
        hs = slice(hh * LANES, (hh + 1) * LANES)
        q = q_ref[:, hs]

        def block(kstart, carry, masked, hs=hs, q=q):
            m, l, acc = carry
            kb = k_ref[pl.ds(kstart, tk), hs]
            s = _dot_nt(q, kb)
            if masked:
                s = jnp.where(diag_mask, s, -jnp.inf)
            m_new = jnp.maximum(m, jnp.max(s, axis=-1, keepdims=True))
            alpha = jnp.exp(m - m_new)
            pr = jnp.exp(s - m_new)
            l = alpha * l + jnp.sum(pr, axis=-1, keepdims=True)
            vb = v_ref[pl.ds(kstart, tk), :]
            acc = alpha * acc + _dot(pr.astype(BF16), vb)
            return m_new, l, acc

        init = (jnp.full((tq, 1), -jnp.inf, F32), jnp.zeros((tq, 1), F32),
                jnp.zeros((tq, LANES), F32))
        carry = lax.fori_loop(
            0, i, lambda j, cr: block(pl.multiple_of(j * tk, tk), cr, False), init)
        _, l, acc = block(pl.multiple_of(i * tq, tq), carry, True)
        outs.append(acc / l)
    lane = lax.broadcasted_iota(jnp.int32, (tq, LANES), 1)
    o_ref[...] = jnp.where(lane < V_DIM, outs[0], outs[1]).astype(BF16)


def _mla_attn(q, k, v, batch, seq):
    tq, tk = ATTN_TQ, ATTN_TK
    assert tq == tk
    nq = seq // tq
    pair = 2 * LANES
    return pl.pallas_call(
        functools.partial(_attn_kernel, tq=tq, tk=tk),
        grid=(batch, MLA_HEADS // 2, nq),
        in_specs=[pl.BlockSpec((tq, pair), lambda b, j, i: (b * nq + i, j)),
                  pl.BlockSpec((seq, pair), lambda b, j, i: (b, j)),
                  pl.BlockSpec((seq, LANES), lambda b, j, i: (b, j))],
        out_specs=pl.BlockSpec((tq, LANES), lambda b, j, i: (b * nq + i, j)),
        out_shape=jax.ShapeDtypeStruct((batch * seq, D_ATTN), BF16),
        compiler_params=_params(("parallel", "parallel", "arbitrary")),
        name="mla_attn",
    )(q, k, v)


def _mem_kv_kernel(m_ref, g_ref, wk_ref, wv_ref, k_ref, v_ref):
    h = _rms(m_ref[...], g_ref[...]).astype(BF16)
    k_ref[...] = _dot(h, wk_ref[...]).astype(BF16)
    v_ref[...] = _dot(h, wv_ref[...]).astype(BF16)


def _mem_kv(mem, g, wk, wv):
    depth = g.shape[0]
    batch, ml, _ = mem.shape
    lay = lambda shape: pl.BlockSpec((None,) + shape, lambda l, b: (l, 0, 0))
    out = pl.BlockSpec((None, None, ml, D_MODEL), lambda l, b: (l, b, 0, 0))
    return pl.pallas_call(
        _mem_kv_kernel,
        grid=(depth, batch),
        in_specs=[pl.BlockSpec((None, ml, D_MODEL), lambda l, b: (b, 0, 0)),
                  lay((1, D_MODEL)), lay((D_MODEL, D_MODEL)), lay((D_MODEL, D_MODEL))],
        out_specs=[out, out],
        out_shape=[jax.ShapeDtypeStruct((depth, batch, ml, D_MODEL), BF16)] * 2,
        compiler_params=_params(("arbitrary", "arbitrary")),
        name="mem_kv",
    )(mem, g, wk, wv)


def _mix_out_mem_kernel(x_ref, ys_ref, ya_ref, ag_ref, wo_ref, mg_ref, wq_ref, km_ref, vm_ref,
                        wmo_ref, o_ref):
    ya = _rms(ya_ref[...].astype(F32), ag_ref[...]).astype(BF16)
    x1 = x_ref[...] + _dot(ys_ref[...], wo_ref[0:D_SSM, :]) + _dot(ya, wo_ref[D_SSM:, :])
    hq = _rms(x1, mg_ref[...]).astype(BF16)
    qm = (_dot(hq, wq_ref[...]) * (MEM_HEAD_DIM ** -0.5)).astype(BF16)
    outs = []
    for hd in range(MEM_HEADS):
        hs = slice(hd * MEM_HEAD_DIM, (hd + 1) * MEM_HEAD_DIM)
        s = _dot_nt(qm[:, hs], km_ref[:, hs])
        m = jnp.max(s, axis=-1, keepdims=True)
        pr = jnp.exp(s - m)
        l = jnp.sum(pr, axis=-1, keepdims=True)
        outs.append(_dot(pr.astype(BF16), vm_ref[:, hs]) / l)
    o = jnp.concatenate(outs, axis=-1).astype(BF16)
    o_ref[...] = x1 + _dot(o, wmo_ref[...])


def _mix_out_mem(x2d, ys, ya, kmem, vmem, l, p, batch, seq, tm):
    ns = seq // tm
    ml = kmem.shape[2]
    row = lambda w: pl.BlockSpec((tm, w), lambda b, i: (b * ns + i, 0))
    lay = lambda shape: _const_spec((None,) + shape, (l, 0, 0))
    memspec = pl.BlockSpec((None, None, ml, D_MODEL), lambda b, i: (l, b, 0, 0))
    return pl.pallas_call(
        _mix_out_mem_kernel,
        grid=(batch, ns),
        in_specs=[row(D_MODEL), row(D_SSM), row(D_ATTN), lay((1, D_ATTN)),
                  lay((D_SSM + D_ATTN, D_MODEL)), lay((1, D_MODEL)), lay((D_MODEL, D_MODEL)),
                  memspec, memspec, lay((D_MODEL, D_MODEL))],
        out_specs=row(D_MODEL),
        out_shape=jax.ShapeDtypeStruct(x2d.shape, F32),
        compiler_params=_params(("parallel", "parallel")),
        name="mix_out_mem",
    )(x2d, ys, ya, p["attn_out_norm"], p["w_out"], p["norm_mem_q"], p["w_mq"], kmem, vmem,
      p["w_mo"])


def _ffn_kernel(x_ref, xh_ref, g_ref, wup_ref, cw_ref, cb_ref, wdn_ref, fg_ref, o_ref, *, final):
    i = pl.program_id(1)
    x = x_ref[...]
    xh = jnp.where(i > 0, xh_ref[...], 0.0)
    h = _rms(jnp.concatenate([xh, x], axis=0), g_ref[...]).astype(BF16)

    def conv(u, cols):
        w = cw_ref[:, cols]
        out = cb_ref[:, cols] + w[FFN_CONV - 1:FFN_CONV, :] * u[SUBLANES:, :]
        for j in range(FFN_CONV - 1):
            shift = FFN_CONV - 1 - j
            out = out + w[j:j + 1, :] * pltpu.roll(u, shift, 0)[SUBLANES:, :]
        return out

    acc = x
    for c in range(D_FF // FFN_CHUNK):
        gcols = slice(c * FFN_CHUNK, (c + 1) * FFN_CHUNK)
        vcols = slice(D_FF + c * FFN_CHUNK, D_FF + (c + 1) * FFN_CHUNK)
        gate = conv(_dot(h, wup_ref[:, gcols]), gcols)
        val = conv(_dot(h, wup_ref[:, vcols]), vcols)
        act = (_silu(gate) * val).astype(BF16)
        acc = acc + _dot(act, wdn_ref[gcols, :])
    if final:
        acc = _rms(acc, fg_ref[...])
    o_ref[...] = acc


def _ffn(x2d, l, p, final_norm, batch, seq, tm, final):
    ns = seq // tm
    halo_blocks_per_seq = seq // SUBLANES
    row = pl.BlockSpec((tm, D_MODEL), lambda b, i: (b * ns + i, 0))
    halo = pl.BlockSpec(
        (SUBLANES, D_MODEL),
        lambda b, i: (jnp.maximum(b * halo_blocks_per_seq + i * (tm // SUBLANES) - 1, 0), 0))
    lay = lambda shape: _const_spec((None,) + shape, (l, 0, 0))
    return pl.pallas_call(
        functools.partial(_ffn_kernel, final=final),
        grid=(batch, ns),
        in_specs=[row, halo, lay((1, D_MODEL)), lay((D_MODEL, 2 * D_FF)), lay((FFN_CONV, 2 * D_FF)),
                  lay((1, 2 * D_FF)), lay((D_FF, D_MODEL)), _const_spec((1, D_MODEL), (0, 0))],
        out_specs=row,
        out_shape=jax.ShapeDtypeStruct(x2d.shape, F32),
        compiler_params=_params(("parallel", "parallel")),
        name="ffn_final" if final else "ffn",
    )(x2d, x2d, p["norm_ffn"], p["w_up"], p["ffn_conv_w"], p["ffn_conv_b"], p["w_down"], final_norm)


def _prep_params(norm_mix, w_in, ssm_conv_w, ssm_conv_b, dt_bias, a_log, d_skip, ssm_norm, q_norm,
                 w_uq, kv_norm, w_ukv, attn_out_norm, w_out, norm_mem_q, w_mq, w_mo, norm_ffn, w_up,
                 ffn_conv_w, ffn_conv_b, w_down):
    depth = w_in.shape[0]
    o1 = D_SSM
    o2 = o1 + CONV_CH
    o3 = o2 + SSM_HEADS
    o4 = o3 + Q_LORA
    o5 = o4 + KV_LORA
    vec = lambda a: a.reshape(depth, 1, -1)
    w_main = jnp.concatenate([w_in[..., :o2], w_in[..., o3:o5]], axis=-1).astype(BF16)
    zeros = lambda w: jnp.zeros((depth, D_MODEL, w), w_in.dtype)
    w_small = jnp.concatenate([zeros(ROPE_LO), w_in[..., o5:], w_in[..., o2:o3],
                               zeros(LANES - DT_HI)], axis=-1).astype(BF16)
    uq = w_uq.reshape(depth, Q_LORA, MLA_HEADS, QK_NOPE + QK_ROPE)
    uq = jnp.pad(uq, ((0, 0), (0, 0), (0, 0), (0, LANES - QK_NOPE - QK_ROPE)))
    ukv = w_ukv.reshape(depth, KV_LORA, MLA_HEADS, QK_NOPE + V_DIM)
    uk = jnp.pad(ukv[..., :QK_NOPE], ((0, 0), (0, 0), (0, 0), (0, LANES - QK_NOPE)))
    uv = ukv[..., QK_NOPE:]
    lane_pad = lambda a: jnp.pad(a, ((0, 0), (DT_LO, LANES - DT_HI))).reshape(depth, 1, LANES)
    return {
        "norm_mix": vec(norm_mix), "w_main": w_main, "w_small": w_small,
        "q_norm": vec(q_norm), "w_uq": uq.reshape(depth, Q_LORA, MLA_HEADS * LANES).astype(BF16),
        "kv_norm": vec(kv_norm), "w_uk": uk.reshape(depth, KV_LORA, MLA_HEADS * LANES).astype(BF16),
        "w_uv": uv.reshape(depth, KV_LORA, D_ATTN).astype(BF16),
        "ssm_conv_w": ssm_conv_w, "ssm_conv_b": vec(ssm_conv_b),
        "dt_bias": lane_pad(dt_bias), "a_log": lane_pad(a_log),
        "d_skip": jnp.repeat(d_skip, SSM_HEAD_DIM, axis=-1).reshape(depth, 1, D_SSM),
        "ssm_norm": vec(ssm_norm), "attn_out_norm": vec(attn_out_norm),
        "w_out": w_out.astype(BF16), "norm_mem_q": vec(norm_mem_q), "w_mq": w_mq.astype(BF16),
        "w_mo": w_mo.astype(BF16), "norm_ffn": vec(norm_ffn), "w_up": w_up.astype(BF16),
        "ffn_conv_w": ffn_conv_w, "ffn_conv_b": vec(ffn_conv_b), "w_down": w_down.astype(BF16),
    }


def kernel(x, mem, positions, norm_mix, w_in, ssm_conv_w, ssm_conv_b, dt_bias, a_log, d_skip,
           ssm_norm, q_norm, w_uq, kv_norm, w_ukv, attn_out_norm, w_out, norm_mem_q, norm_mem_kv,
           w_mq, w_mk, w_mv, w_mo, norm_ffn, w_up, ffn_conv_w, ffn_conv_b, w_down, final_norm):
    batch, seq, _ = x.shape
    depth = w_in.shape[0]
    p = _prep_params(norm_mix, w_in, ssm_conv_w, ssm_conv_b, dt_bias, a_log, d_skip, ssm_norm,
                     q_norm, w_uq, kv_norm, w_ukv, attn_out_norm, w_out, norm_mem_q, w_mq, w_mo,
                     norm_ffn, w_up, ffn_conv_w, ffn_conv_b, w_down)
    ctab, stab = _rope_tables(positions)
    kmem, vmem = _mem_kv(mem, norm_mem_kv.reshape(depth, 1, D_MODEL), w_mk.astype(BF16),
                         w_mv.astype(BF16))
    fnorm = final_norm.reshape(1, D_MODEL)
    tm_in = min(256, seq)
    tm = min(512, seq)
    tc = min(256, seq)
    x2d = x.reshape(batch * seq, D_MODEL)
    for l in range(depth):
        z, xbc, sm, q, k, v = _mix_in(x2d, l, p, ctab, stab, tm_in)
        ys = _ssd(xbc, z, sm, l, p, batch, seq, tc)
        ya = _mla_attn(q, k, v, batch, seq)
        x2d = _mix_out_mem(x2d, ys, ya, kmem, vmem, l, p, batch, seq, tm)
        x2d = _ffn(x2d, l, p, fnorm, batch, seq, tm, final=(l == depth - 1))
    return x2d.reshape(batch, seq, D_MODEL)
```

```python
import functools
import math

import jax
import jax.numpy as jnp
from jax import lax
from jax.experimental import pallas as pl
from jax.experimental.pallas import tpu as pltpu

F32 = jnp.float32
BF16 = jnp.bfloat16

D_MODEL = 1024
EPS = 1e-6
SSM_HEADS = 16
SSM_HEAD_DIM = 64
D_SSM = SSM_HEADS * SSM_HEAD_DIM
SSM_GROUPS = 4
SSM_STATE = 128
SSM_CONV = 4
SSM_CHUNK = 128
CONV_CH = D_SSM + 2 * SSM_GROUPS * SSM_STATE
MLA_HEADS = 16
QK_NOPE = 64
QK_ROPE = 32
V_DIM = 64
Q_LORA = 384
KV_LORA = 256
D_ATTN = MLA_HEADS * V_DIM
ROPE_THETA = 10000.0
MEM_HEADS = 4
MEM_HEAD_DIM = D_MODEL // MEM_HEADS
D_FF = 2816
FFN_CONV = 3

LANES = 128
SUBLANES = 8
VMEM_LIMIT = 56 * 1024 * 1024

ROPE_LO = QK_NOPE
ROPE_MID = QK_NOPE + QK_ROPE // 2
ROPE_HI = QK_NOPE + QK_ROPE
DT_LO = ROPE_HI
DT_HI = ROPE_HI + SSM_HEADS

FFN_CHUNK = 256
CONV_CHUNK = 512
assert SSM_CONV == 4 and CONV_CH % CONV_CHUNK == 0
ATTN_SUB = 256
ATTN_STEP_SUBS = 2
ATTN_HEADS = 4


def _dot(a, b):
    return jnp.dot(a, b, preferred_element_type=F32)


def _dot_nt(a, b):
    return lax.dot_general(a, b, (((1,), (1,)), ((), ())), preferred_element_type=F32)


def _rms(xf, g):
    var = jnp.mean(xf * xf, axis=-1, keepdims=True)
    return xf * lax.rsqrt(var + EPS) * g


def _silu(x):
    return x * (1.0 / (1.0 + jnp.exp(-x)))


def _softplus(x):
    return jnp.maximum(x, 0.0) + jnp.log1p(jnp.exp(-jnp.abs(x)))


def _rope128(t, c, s, lane):
    rot = jnp.where(lane < ROPE_MID, pltpu.roll(t, LANES - QK_ROPE // 2, 1),
                    pltpu.roll(t, QK_ROPE // 2, 1))
    return t * c + rot * s


def _params(sem):
    return pltpu.CompilerParams(dimension_semantics=sem, vmem_limit_bytes=VMEM_LIMIT)


def _const_spec(shape, index):
    return pl.BlockSpec(shape, lambda *_: index, pipeline_mode=pl.Buffered(1))


def _rope_kernel(pos_ref, inv_ref, cos_ref, sin_ref):
    ang = pos_ref[...] * inv_ref[...]
    cos_ref[...] = jnp.cos(ang)
    sin_ref[...] = jnp.sin(ang)


def _rope_tables(positions):
    t = positions.size
    half = QK_ROPE // 2
    per_row = LANES // half
    inv_freq = 1.0 / (ROPE_THETA ** (jnp.arange(0, QK_ROPE, 2, dtype=F32) / QK_ROPE))
    posd = jnp.repeat(positions.reshape(t).astype(F32), half).reshape(t // per_row, LANES)
    invd = jnp.tile(inv_freq, per_row).reshape(1, LANES)
    rows = t // per_row
    cosd, sind = pl.pallas_call(
        _rope_kernel,
        out_shape=(jax.ShapeDtypeStruct((rows, LANES), F32),) * 2,
        name="rope_tables",
    )(posd, invd)
    cos = cosd.reshape(t, half)
    sin = sind.reshape(t, half)
    ctab = jnp.concatenate([jnp.ones((t, QK_NOPE), F32), cos, cos,
                            jnp.zeros((t, LANES - ROPE_HI), F32)], axis=-1)
    stab = jnp.concatenate([jnp.zeros((t, QK_NOPE), F32), -sin, sin,
                            jnp.zeros((t, LANES - ROPE_HI), F32)], axis=-1)
    return ctab, stab


_O_Z = 0
_O_XBC = D_SSM
_O_CQ = _O_XBC + CONV_CH
_O_CKV = _O_CQ + Q_LORA
_W_MAIN = _O_CKV + KV_LORA
_QK_SCALE = (QK_NOPE + QK_ROPE) ** -0.5 * math.log2(math.e)


def _mix_in_kernel(x_ref, xh_ref, g_ref, wm_ref, ws_ref, cw_ref, cb_ref, qg_ref, wuq_ref, kg_ref,
                   wuk_ref, wuv_ref, c_ref, s_ref, z_ref, xbc_ref, sm_ref, q_ref, k_ref, v_ref,
                   *, tiles_per_seq):
    first = pl.program_id(0) % tiles_per_seq == 0
    xh = jnp.where(first, 0.0, xh_ref[...])
    h_full = _rms(jnp.concatenate([xh, x_ref[...]], axis=0), g_ref[...]).astype(BF16)
    h = h_full[SUBLANES:, :]

    def xbc_dot(cc):
        lo = _O_XBC + cc * CONV_CHUNK
        return _dot(h_full, wm_ref[:, lo:lo + CONV_CHUNK])

    def conv_store(cc, u):
        cols = slice(cc * CONV_CHUNK, (cc + 1) * CONV_CHUNK)
        cw = cw_ref[:, cols]
        u1 = pltpu.roll(u, 1, 0)
        near = cw[3:4, :] * u + cw[2:3, :] * u1
        far = cw[1:2, :] * u + cw[0:1, :] * u1
        acc = cb_ref[:, cols] + near[SUBLANES:, :] + pltpu.roll(far, 2, 0)[SUBLANES:, :]
        xbc_ref[:, cols] = _silu(acc).astype(BF16)

    u = xbc_dot(0)
    z_ref[...] = _dot(h, wm_ref[:, _O_Z:_O_XBC]).astype(BF16)
    conv_store(0, u)
    u = xbc_dot(1)
    cq = _dot(h, wm_ref[:, _O_CQ:_O_CKV])
    ckv = _dot(h, wm_ref[:, _O_CKV:_W_MAIN])
    sm = _dot(h, ws_ref[...])
    sm_ref[...] = sm
    c = c_ref[...]
    s = s_ref[...]
    lane = lax.broadcasted_iota(jnp.int32, c.shape, 1)
    cqn = _rms(cq, qg_ref[...]).astype(BF16)
    qq = _dot(cqn, wuq_ref[...])
    conv_store(1, u)
    u = xbc_dot(2)
    cs = c * _QK_SCALE
    ss = s * _QK_SCALE
    for hd in range(MLA_HEADS):
        sl = slice(hd * LANES, (hd + 1) * LANES)
        q_ref[:, sl] = _rope128(qq[:, sl], cs, ss, lane).astype(BF16)
    kpe = _rope128(sm, c, s, lane)
    kpe = jnp.where((lane >= ROPE_LO) & (lane < ROPE_HI), kpe, 0.0)
    ckvn = _rms(ckv, kg_ref[...]).astype(BF16)
    kk = _dot(ckvn, wuk_ref[...])
    conv_store(2, u)
    u = xbc_dot(3)
    for hd in range(MLA_HEADS):
        sl = slice(hd * LANES, (hd + 1) * LANES)
        k_ref[:, sl] = (kk[:, sl] + kpe).astype(BF16)
    v_ref[...] = _dot(ckvn, wuv_ref[...]).astype(BF16)
    conv_store(3, u)


def _mix_in(x2d, l, p, ctab, stab, seq, tm):
    t = x2d.shape[0]
    hp = MLA_HEADS * LANES
    row = lambda w: pl.BlockSpec((tm, w), lambda i: (i, 0))
    halo = pl.BlockSpec((SUBLANES, D_MODEL),
                        lambda i: (jnp.maximum(i * (tm // SUBLANES) - 1, 0), 0))
    lay = lambda shape: _const_spec((None,) + shape, (l, 0, 0))
    return pl.pallas_call(
        functools.partial(_mix_in_kernel, tiles_per_seq=seq // tm),
        grid=(t // tm,),
        in_specs=[row(D_MODEL), halo, lay((1, D_MODEL)), lay((D_MODEL, _W_MAIN)),
                  lay((D_MODEL, LANES)), lay((SSM_CONV, CONV_CH)), lay((1, CONV_CH)),
                  lay((1, Q_LORA)), lay((Q_LORA, hp)), lay((1, KV_LORA)), lay((KV_LORA, hp)),
                  lay((KV_LORA, D_ATTN)), row(LANES), row(LANES)],
        out_specs=[row(D_SSM), row(CONV_CH), row(LANES), row(hp), row(hp), row(D_ATTN)],
        out_shape=[jax.ShapeDtypeStruct((t, D_SSM), BF16), jax.ShapeDtypeStruct((t, CONV_CH), BF16),
                   jax.ShapeDtypeStruct((t, LANES), F32), jax.ShapeDtypeStruct((t, hp), BF16),
                   jax.ShapeDtypeStruct((t, hp), BF16), jax.ShapeDtypeStruct((t, D_ATTN), BF16)],
        compiler_params=_params(("parallel",)),
        name="mix_in",
    )(x2d, x2d, p["norm_mix"], p["w_main"], p["w_small"], p["ssm_conv_w"], p["ssm_conv_b"],
      p["q_norm"], p["w_uq"], p["kv_norm"], p["w_uk"], p["w_uv"], ctab, stab)


def _split3(a):
    hi = a.astype(BF16)
    r = a - hi.astype(F32)
    mid = r.astype(BF16)
    lo = (r - mid.astype(F32)).astype(BF16)
    return hi, mid, lo


def _ssd_kernel(xbc_ref, z_ref, sm_ref, dtb_ref, alog_ref, dsk_ref, g_ref, o_ref, state_ref, *, tc):
    i = pl.program_id(1)

    @pl.when(i == 0)
    def _():
        state_ref[...] = jnp.zeros_like(state_ref)

    xc = xbc_ref[...].astype(F32)

    n = SSM_CHUNK
    ri = lax.broadcasted_iota(jnp.int32, (n, n), 0)
    ci = lax.broadcasted_iota(jnp.int32, (n, n), 1)
    causal = ri >= ci
    tri = jnp.where(causal, 1.0, 0.0).astype(BF16)
    lane = lax.broadcasted_iota(jnp.int32, (1, LANES), 1)
    dt_lanes = (lane >= DT_LO) & (lane < DT_HI)
    a_tile = jnp.where(dt_lanes, -jnp.exp(alog_ref[...]), 0.0)
    dtb = dtb_ref[...]
    bo = D_SSM
    co = D_SSM + SSM_GROUPS * SSM_STATE
    heads_per_group = SSM_HEADS // SSM_GROUPS

    for c in range(tc // n):
        rows = slice(c * n, (c + 1) * n)
        xs = xc[rows, 0:D_SSM]
        dtf = _softplus(sm_ref[rows, :] + dtb)
        ad = dtf * a_tile
        hi, mid, lo = _split3(ad)
        cs = _dot(tri, hi) + _dot(tri, mid) + _dot(tri, lo)
        cs_t = cs.T
        dt_t = dtf.T
        ys = []
        for g in range(SSM_GROUPS):
            bg = xc[rows, bo + g * SSM_STATE: bo + (g + 1) * SSM_STATE]
            cg = xc[rows, co + g * SSM_STATE: co + (g + 1) * SSM_STATE]
            bg_t = bg.T
            cg16 = cg.astype(BF16)
            cb = _dot_nt(cg16, bg.astype(BF16))
            for r in range(heads_per_group):
                hd = g * heads_per_group + r
                ln = DT_LO + hd
                col = cs[:, ln:ln + 1]
                row = cs_t[ln:ln + 1, :]
                dt_row = dt_t[ln:ln + 1, :]
                last = cs[n - 1:n, ln:ln + 1]
                lmat = jnp.exp(jnp.where(causal, col - row, -jnp.inf))
                mm = (cb * lmat * dt_row).astype(BF16)
                xh = xs[:, hd * SSM_HEAD_DIM:(hd + 1) * SSM_HEAD_DIM].astype(BF16)
                prev = state_ref[hd]
                y_h = _dot(mm, xh) + jnp.exp(col) * _dot(cg16, prev.astype(BF16))
                w_row = jnp.exp(last - row) * dt_row
                st_new = _dot((bg_t * w_row).astype(BF16), xh)
                state_ref[hd] = prev * jnp.exp(last) + st_new
                ys.append(y_h)
        y = jnp.concatenate(ys, axis=-1) + xs * dsk_ref[...]
        y = y * _silu(z_ref[rows, :].astype(F32))
        o_ref[rows, :] = _rms(y, g_ref[...]).astype(BF16)


def _ssd(xbc, z, sm, l, p, batch, seq, tc):
    ns = seq // tc
    main = lambda w: pl.BlockSpec((tc, w), lambda b, i: (b * ns + i, 0))
    lay = lambda shape: _const_spec((None,) + shape, (l, 0, 0))
    return pl.pallas_call(
        functools.partial(_ssd_kernel, tc=tc),
        grid=(batch, ns),
        in_specs=[main(CONV_CH), main(D_SSM), main(LANES), lay((1, LANES)), lay((1, LANES)),
                  lay((1, D_SSM)), lay((1, D_SSM))],
        out_specs=main(D_SSM),
        out_shape=jax.ShapeDtypeStruct((batch * seq, D_SSM), BF16),
        scratch_shapes=[pltpu.VMEM((SSM_HEADS, SSM_STATE, SSM_HEAD_DIM), F32)],
        compiler_params=_params(("arbitrary", "arbitrary")),
        name="ssd",
    )(xbc, z, sm, p["dt_bias"], p["a_log"], p["d_skip"], p["ssm_norm"])


def _attn_tile(q_ref, k_ref, v_ref, o_ref, *, c):
    tq = ATTN_SUB
    ri = lax.broadcasted_iota(jnp.int32, (tq, tq), 0)
    ci = lax.broadcasted_iota(jnp.int32, (tq, tq), 1)
    diag_mask = ri >= ci
    lane = lax.broadcasted_iota(jnp.int32, (tq, ATTN_HEADS * V_DIM), 1)
    for sub in range(ATTN_STEP_SUBS):
        t = c * ATTN_STEP_SUBS + sub
        kv_len = (t + 1) * tq
        rows = slice(sub * tq, (sub + 1) * tq)
        vb = v_ref[0:kv_len, :]
        out = None
        for hh in range(ATTN_HEADS):
            hs = slice(hh * LANES, (hh + 1) * LANES)
            s = _dot_nt(q_ref[rows, hs], k_ref[0:kv_len, hs])
            diag = jnp.where(diag_mask, s[:, t * tq:], -jnp.inf)
            s = diag if t == 0 else jnp.concatenate([s[:, :t * tq], diag], axis=-1)
            m = jnp.max(s, axis=-1, keepdims=True)
            pr = jnp.exp2(s - m)
            l = jnp.sum(pr, axis=-1, keepdims=True)
            o = _dot(pr.astype(BF16), vb) / l
            out = o if hh == 0 else jnp.where(lane >= hh * V_DIM, o, out)
        o_ref[rows, :] = out.astype(BF16)


def _attn_kernel(q_ref, k_ref, v_ref, o_ref, *, nsteps):
    i = pl.program_id(2)
    for c in range(nsteps):
        pl.when(i == c)(functools.partial(_attn_tile, q_ref, k_ref, v_ref, o_ref, c=c))


def _mla_attn(q, k, v, batch, seq):
    tstep = ATTN_SUB * ATTN_STEP_SUBS
    nsteps = seq // tstep
    qw = ATTN_HEADS * LANES
    vw = ATTN_HEADS * V_DIM
    return pl.pallas_call(
        functools.partial(_attn_kernel, nsteps=nsteps),
        grid=(batch, MLA_HEADS // ATTN_HEADS, nsteps),
        in_specs=[pl.BlockSpec((tstep, qw), lambda b, j, i: (b * nsteps + i, j)),
                  pl.BlockSpec((seq, qw), lambda b, j, i: (b, j)),
                  pl.BlockSpec((seq, vw), lambda b, j, i: (b, j))],
        out_specs=pl.BlockSpec((tstep, vw), lambda b, j, i: (b * nsteps + i, j)),
        out_shape=jax.ShapeDtypeStruct((batch * seq, D_ATTN), BF16),
        compiler_params=_params(("parallel", "parallel", "arbitrary")),
        name="mla_attn",
    )(q, k, v)


def _mem_kv_kernel(m_ref, g_ref, wk_ref, wv_ref, k_ref, v_ref):
    h = _rms(m_ref[...], g_ref[...]).astype(BF16)
    k_ref[...] = _dot(h, wk_ref[...]).astype(BF16)
    v_ref[...] = _dot(h, wv_ref[...]).astype(BF16)


def _mem_kv(mem, g, wk, wv):
    depth = g.shape[0]
    batch, ml, _ = mem.shape
    lay = lambda shape: pl.BlockSpec((None,) + shape, lambda l, b: (l, 0, 0))
    out = pl.BlockSpec((None, None, ml, D_MODEL), lambda l, b: (l, b, 0, 0))
    return pl.pallas_call(
        _mem_kv_kernel,
        grid=(depth, batch),
        in_specs=[pl.BlockSpec((None, ml, D_MODEL), lambda l, b: (b, 0, 0)),
                  lay((1, D_MODEL)), lay((D_MODEL, D_MODEL)), lay((D_MODEL, D_MODEL))],
        out_specs=[out, out],
        out_shape=[jax.ShapeDtypeStruct((depth, batch, ml, D_MODEL), BF16)] * 2,
        compiler_params=_params(("arbitrary", "arbitrary")),
        name="mem_kv",
    )(mem, g, wk, wv)


def _mix_out_mem_kernel(x_ref, ys_ref, ya_ref, ag_ref, wo_ref, mg_ref, wq_ref, km_ref, vm_ref,
                        wmo_ref, o_ref):
    ya = _rms(ya_ref[...].astype(F32), ag_ref[...]).astype(BF16)
    x1 = x_ref[...] + _dot(ys_ref[...], wo_ref[0:D_SSM, :]) + _dot(ya, wo_ref[D_SSM:, :])
    hq = _rms(x1, mg_ref[...]).astype(BF16)
    qm = (_dot(hq, wq_ref[...]) * (MEM_HEAD_DIM ** -0.5)).astype(BF16)
    outs = []
    for hd in range(MEM_HEADS):
        hs = slice(hd * MEM_HEAD_DIM, (hd + 1) * MEM_HEAD_DIM)
        s = _dot_nt(qm[:, hs], km_ref[:, hs])
        m = jnp.max(s, axis=-1, keepdims=True)
        pr = jnp.exp(s - m)
        l = jnp.sum(pr, axis=-1, keepdims=True)
        outs.append(_dot(pr.astype(BF16), vm_ref[:, hs]) / l)
    o = jnp.concatenate(outs, axis=-1).astype(BF16)
    o_ref[...] = x1 + _dot(o, wmo_ref[...])


def _mix_out_mem(x2d, ys, ya, kmem, vmem, l, p, batch, seq, tm):
    ns = seq // tm
    ml = kmem.shape[2]
    row = lambda w: pl.BlockSpec((tm, w), lambda b, i: (b * ns + i, 0))
    lay = lambda shape: _const_spec((None,) + shape, (l, 0, 0))
    memspec = pl.BlockSpec((None, None, ml, D_MODEL), lambda b, i: (l, b, 0, 0))
    return pl.pallas_call(
        _mix_out_mem_kernel,
        grid=(batch, ns),
        in_specs=[row(D_MODEL), row(D_SSM), row(D_ATTN), lay((1, D_ATTN)),
                  lay((D_SSM + D_ATTN, D_MODEL)), lay((1, D_MODEL)), lay((D_MODEL, D_MODEL)),
                  memspec, memspec, lay((D_MODEL, D_MODEL))],
        out_specs=row(D_MODEL),
        out_shape=jax.ShapeDtypeStruct(x2d.shape, F32),
        compiler_params=_params(("parallel", "parallel")),
        name="mix_out_mem",
    )(x2d, ys, ya, p["attn_out_norm"], p["w_out"], p["norm_mem_q"], p["w_mq"], kmem, vmem,
      p["w_mo"])


def _ffn_kernel(x_ref, xh_ref, g_ref, wup_ref, cw_ref, cb_ref, wdn_ref, fg_ref, o_ref, *, final):
    i = pl.program_id(1)
    x = x_ref[...]
    xh = jnp.where(i > 0, xh_ref[...], 0.0)
    h = _rms(jnp.concatenate([xh, x], axis=0), g_ref[...]).astype(BF16)

    def conv(u, cols):
        w = cw_ref[:, cols]
        out = cb_ref[:, cols] + w[FFN_CONV - 1:FFN_CONV, :] * u[SUBLANES:, :]
        for j in range(FFN_CONV - 1):
            shift = FFN_CONV - 1 - j
            out = out + w[j:j + 1, :] * pltpu.roll(u, shift, 0)[SUBLANES:, :]
        return out

    nchunks = D_FF // FFN_CHUNK
    gcols = lambda c: slice(c * FFN_CHUNK, (c + 1) * FFN_CHUNK)
    vcols = lambda c: slice(D_FF + c * FFN_CHUNK, D_FF + (c + 1) * FFN_CHUNK)
    up = lambda c: (_dot(h, wup_ref[:, gcols(c)]), _dot(h, wup_ref[:, vcols(c)]))

    acc = x
    nxt = up(0)
    for c in range(nchunks):
        ug, uv = nxt
        if c + 1 < nchunks:
            nxt = up(c + 1)
        act = (_silu(conv(ug, gcols(c))) * conv(uv, vcols(c))).astype(BF16)
        acc = acc + _dot(act, wdn_ref[gcols(c), :])
    if final:
        acc = _rms(acc, fg_ref[...])
    o_ref[...] = acc


def _ffn(x2d, l, p, final_norm, batch, seq, tm, final):
    ns = seq // tm
    halo_blocks_per_seq = seq // SUBLANES
    row = pl.BlockSpec((tm, D_MODEL), lambda b, i: (b * ns + i, 0))
    halo = pl.BlockSpec(
        (SUBLANES, D_MODEL),
        lambda b, i: (jnp.maximum(b * halo_blocks_per_seq + i * (tm // SUBLANES) - 1, 0), 0))
    lay = lambda shape: _const_spec((None,) + shape, (l, 0, 0))
    return pl.pallas_call(
        functools.partial(_ffn_kernel, final=final),
        grid=(batch, ns),
        in_specs=[row, halo, lay((1, D_MODEL)), lay((D_MODEL, 2 * D_FF)), lay((FFN_CONV, 2 * D_FF)),
                  lay((1, 2 * D_FF)), lay((D_FF, D_MODEL)), _const_spec((1, D_MODEL), (0, 0))],
        out_specs=row,
        out_shape=jax.ShapeDtypeStruct(x2d.shape, F32),
        compiler_params=_params(("parallel", "parallel")),
        name="ffn_final" if final else "ffn",
    )(x2d, x2d, p["norm_ffn"], p["w_up"], p["ffn_conv_w"], p["ffn_conv_b"], p["w_down"], final_norm)


def _prep_params(norm_mix, w_in, ssm_conv_w, ssm_conv_b, dt_bias, a_log, d_skip, ssm_norm, q_norm,
                 w_uq, kv_norm, w_ukv, attn_out_norm, w_out, norm_mem_q, w_mq, w_mo, norm_ffn, w_up,
                 ffn_conv_w, ffn_conv_b, w_down):
    depth = w_in.shape[0]
    o1 = D_SSM
    o2 = o1 + CONV_CH
    o3 = o2 + SSM_HEADS
    o4 = o3 + Q_LORA
    o5 = o4 + KV_LORA
    vec = lambda a: a.reshape(depth, 1, -1)
    w_main = jnp.concatenate([w_in[..., :o2], w_in[..., o3:o5]], axis=-1).astype(BF16)
    zeros = lambda w: jnp.zeros((depth, D_MODEL, w), w_in.dtype)
    w_small = jnp.concatenate([zeros(ROPE_LO), w_in[..., o5:], w_in[..., o2:o3],
                               zeros(LANES - DT_HI)], axis=-1).astype(BF16)
    uq = w_uq.reshape(depth, Q_LORA, MLA_HEADS, QK_NOPE + QK_ROPE)
    uq = jnp.pad(uq, ((0, 0), (0, 0), (0, 0), (0, LANES - QK_NOPE - QK_ROPE)))
    ukv = w_ukv.reshape(depth, KV_LORA, MLA_HEADS, QK_NOPE + V_DIM)
    uk = jnp.pad(ukv[..., :QK_NOPE], ((0, 0), (0, 0), (0, 0), (0, LANES - QK_NOPE)))
    uv = ukv[..., QK_NOPE:]
    lane_pad = lambda a: jnp.pad(a, ((0, 0), (DT_LO, LANES - DT_HI))).reshape(depth, 1, LANES)
    return {
        "norm_mix": vec(norm_mix), "w_main": w_main, "w_small": w_small,
        "q_norm": vec(q_norm), "w_uq": uq.reshape(depth, Q_LORA, MLA_HEADS * LANES).astype(BF16),
        "kv_norm": vec(kv_norm), "w_uk": uk.reshape(depth, KV_LORA, MLA_HEADS * LANES).astype(BF16),
        "w_uv": uv.reshape(depth, KV_LORA, D_ATTN).astype(BF16),
        "ssm_conv_w": ssm_conv_w, "ssm_conv_b": vec(ssm_conv_b),
        "dt_bias": lane_pad(dt_bias), "a_log": lane_pad(a_log),
        "d_skip": jnp.repeat(d_skip, SSM_HEAD_DIM, axis=-1).reshape(depth, 1, D_SSM),
        "ssm_norm": vec(ssm_norm), "attn_out_norm": vec(attn_out_norm),
        "w_out": w_out.astype(BF16), "norm_mem_q": vec(norm_mem_q), "w_mq": w_mq.astype(BF16),
        "w_mo": w_mo.astype(BF16), "norm_ffn": vec(norm_ffn), "w_up": w_up.astype(BF16),
        "ffn_conv_w": ffn_conv_w, "ffn_conv_b": vec(ffn_conv_b), "w_down": w_down.astype(BF16),
    }


def kernel(x, mem, positions, norm_mix, w_in, ssm_conv_w, ssm_conv_b, dt_bias, a_log, d_skip,
           ssm_norm, q_norm, w_uq, kv_norm, w_ukv, attn_out_norm, w_out, norm_mem_q, norm_mem_kv,
           w_mq, w_mk, w_mv, w_mo, norm_ffn, w_up, ffn_conv_w, ffn_conv_b, w_down, final_norm):
    batch, seq, _ = x.shape
    depth = w_in.shape[0]
    p = _prep_params(norm_mix, w_in, ssm_conv_w, ssm_conv_b, dt_bias, a_log, d_skip, ssm_norm,
                     q_norm, w_uq, kv_norm, w_ukv, attn_out_norm, w_out, norm_mem_q, w_mq, w_mo,
                     norm_ffn, w_up, ffn_conv_w, ffn_conv_b, w_down)
    ctab, stab = _rope_tables(positions)
    kmem, vmem = _mem_kv(mem, norm_mem_kv.reshape(depth, 1, D_MODEL), w_mk.astype(BF16),
                         w_mv.astype(BF16))
    fnorm = final_norm.reshape(1, D_MODEL)
    tm_in = min(256, seq)
    tm = min(512, seq)
    tc = min(256, seq)
    x2d = x.reshape(batch * seq, D_MODEL)
    for l in range(depth):
        z, xbc, sm, q, k, v = _mix_in(x2d, l, p, ctab, stab, seq, tm_in)
        ys = _ssd(xbc, z, sm, l, p, batch, seq, tc)
        ya = _mla_attn(q, k, v, batch, seq)
        x2d = _mix_out_mem(x2d, ys, ya, kmem, vmem, l, p, batch, seq, tm)
        x2d = _ffn(x2d, l, p, fnorm, batch, seq, tm, final=(l == depth - 1))
    return x2d.reshape(batch, seq, D_MODEL)
```

```python
import functools
import math

import jax
import jax.numpy as jnp
from jax import lax
from jax.experimental import pallas as pl
from jax.experimental.pallas import tpu as pltpu

F32 = jnp.float32
BF16 = jnp.bfloat16

D_MODEL = 1024
EPS = 1e-6
SSM_HEADS = 16
SSM_HEAD_DIM = 64
D_SSM = SSM_HEADS * SSM_HEAD_DIM
SSM_GROUPS = 4
SSM_STATE = 128
SSM_CONV = 4
SSM_CHUNK = 128
CONV_CH = D_SSM + 2 * SSM_GROUPS * SSM_STATE
MLA_HEADS = 16
QK_NOPE = 64
QK_ROPE = 32
V_DIM = 64
Q_LORA = 384
KV_LORA = 256
D_ATTN = MLA_HEADS * V_DIM
ROPE_THETA = 10000.0
MEM_HEADS = 4
MEM_HEAD_DIM = D_MODEL // MEM_HEADS
D_FF = 2816
FFN_CONV = 3

LANES = 128
SUBLANES = 8
MXU_TILE = 256
VMEM_LIMIT = 56 * 1024 * 1024

ROPE_LO = QK_NOPE
ROPE_MID = QK_NOPE + QK_ROPE // 2
ROPE_HI = QK_NOPE + QK_ROPE
DT_LO = ROPE_HI
DT_HI = ROPE_HI + SSM_HEADS

FFN_CHUNK = 256
CONV_CHUNK = 512
assert SSM_CONV == 4 and CONV_CH % CONV_CHUNK == 0
ATTN_SUB = 256
ATTN_STEP_SUBS = 2
ATTN_HEADS = 8
PV_WIDTH = 256


def _dot(a, b):
    return jnp.dot(a, b, preferred_element_type=F32)


def _dot_nt(a, b):
    return lax.dot_general(a, b, (((1,), (1,)), ((), ())), preferred_element_type=F32)


def _rms(xf, g):
    var = jnp.mean(xf * xf, axis=-1, keepdims=True)
    return xf * lax.rsqrt(var + EPS) * g


def _silu(x):
    return x * (1.0 / (1.0 + jnp.exp(-x)))


def _softplus(x):
    return jnp.maximum(x, 0.0) + jnp.log1p(jnp.exp(-jnp.abs(x)))


def _rope128(t, c, s, lane):
    rot = jnp.where(lane < ROPE_MID, pltpu.roll(t, LANES - QK_ROPE // 2, 1),
                    pltpu.roll(t, QK_ROPE // 2, 1))
    return t * c + rot * s


def _params(sem):
    return pltpu.CompilerParams(dimension_semantics=sem, vmem_limit_bytes=VMEM_LIMIT)


def _const_spec(shape, index):
    return pl.BlockSpec(shape, lambda *_: index, pipeline_mode=pl.Buffered(1))


def _rope_kernel(pos_ref, inv_ref, cos_ref, sin_ref):
    ang = pos_ref[...] * inv_ref[...]
    cos_ref[...] = jnp.cos(ang)
    sin_ref[...] = jnp.sin(ang)


def _rope_tables(positions):
    t = positions.size
    half = QK_ROPE // 2
    per_row = LANES // half
    inv_freq = 1.0 / (ROPE_THETA ** (jnp.arange(0, QK_ROPE, 2, dtype=F32) / QK_ROPE))
    posd = jnp.repeat(positions.reshape(t).astype(F32), half).reshape(t // per_row, LANES)
    invd = jnp.tile(inv_freq, per_row).reshape(1, LANES)
    rows = t // per_row
    cosd, sind = pl.pallas_call(
        _rope_kernel,
        out_shape=(jax.ShapeDtypeStruct((rows, LANES), F32),) * 2,
        name="rope_tables",
    )(posd, invd)
    cos = cosd.reshape(t, half)
    sin = sind.reshape(t, half)
    ctab = jnp.concatenate([jnp.ones((t, QK_NOPE), F32), cos, cos,
                            jnp.zeros((t, LANES - ROPE_HI), F32)], axis=-1)
    stab = jnp.concatenate([jnp.zeros((t, QK_NOPE), F32), -sin, sin,
                            jnp.zeros((t, LANES - ROPE_HI), F32)], axis=-1)
    return ctab, stab


_O_Z = 0
_O_XBC = D_SSM
_O_CQ = _O_XBC + CONV_CH
_O_CKV = _O_CQ + Q_LORA
_W_MAIN = _O_CKV + KV_LORA
_QK_SCALE = (QK_NOPE + QK_ROPE) ** -0.5 * math.log2(math.e)


def _mix_in_kernel(x_ref, xh_ref, g_ref, wm_ref, ws_ref, cw_ref, cb_ref, qg_ref, wuq_ref, kg_ref,
                   wuk_ref, wuv_ref, c_ref, s_ref, z_ref, xbc_ref, sm_ref, q_ref, k_ref, v_ref,
                   *, tiles_per_seq):
    first = pl.program_id(0) % tiles_per_seq == 0
    xh = jnp.where(first, 0.0, xh_ref[...])
    h_full = _rms(jnp.concatenate([xh, x_ref[...]], axis=0), g_ref[...]).astype(BF16)
    h = h_full[SUBLANES:, :]

    def xbc_dot(cc):
        lo = _O_XBC + cc * CONV_CHUNK
        return _dot(h_full, wm_ref[:, lo:lo + CONV_CHUNK])

    def conv_store(cc, u):
        cols = slice(cc * CONV_CHUNK, (cc + 1) * CONV_CHUNK)
        cw = cw_ref[:, cols]
        u1 = pltpu.roll(u, 1, 0)
        near = cw[3:4, :] * u + cw[2:3, :] * u1
        far = cw[1:2, :] * u + cw[0:1, :] * u1
        acc = cb_ref[:, cols] + near[SUBLANES:, :] + pltpu.roll(far, 2, 0)[SUBLANES:, :]
        xbc_ref[:, cols] = _silu(acc).astype(BF16)

    w = MXU_TILE
    heads_per_tile = w // LANES

    def z_tiles():
        for j in range(D_SSM // w):
            z_ref[:, j * w:(j + 1) * w] = _dot(h, wm_ref[:, _O_Z + j * w:_O_Z + (j + 1) * w]).astype(BF16)

    u = xbc_dot(0)
    z_tiles()
    conv_store(0, u)
    u = xbc_dot(1)
    cq = _dot(h, wm_ref[:, _O_CQ:_O_CKV])
    ckv = _dot(h, wm_ref[:, _O_CKV:_W_MAIN])
    sm = _dot(h, ws_ref[...])
    sm_ref[...] = sm
    c = c_ref[...]
    s = s_ref[...]
    lane = lax.broadcasted_iota(jnp.int32, c.shape, 1)
    cs = c * _QK_SCALE
    ss = s * _QK_SCALE
    cqn = _rms(cq, qg_ref[...]).astype(BF16)
    ckvn = _rms(ckv, kg_ref[...]).astype(BF16)
    kpe = _rope128(sm, c, s, lane)
    kpe = jnp.where((lane >= ROPE_LO) & (lane < ROPE_HI), kpe, 0.0)

    def q_tiles(lo, hi):
        for j in range(lo, hi):
            qq = _dot(cqn, wuq_ref[:, j * w:(j + 1) * w])
            for r in range(heads_per_tile):
                sl = slice(j * w + r * LANES, j * w + (r + 1) * LANES)
                q_ref[:, sl] = _rope128(qq[:, r * LANES:(r + 1) * LANES], cs, ss, lane).astype(BF16)

    def k_tiles(lo, hi):
        for j in range(lo, hi):
            kk = _dot(ckvn, wuk_ref[:, j * w:(j + 1) * w])
            for r in range(heads_per_tile):
                sl = slice(j * w + r * LANES, j * w + (r + 1) * LANES)
                k_ref[:, sl] = (kk[:, r * LANES:(r + 1) * LANES] + kpe).astype(BF16)

    nqk = MLA_HEADS // heads_per_tile
    q_tiles(0, nqk // 2)
    conv_store(1, u)
    u = xbc_dot(2)
    q_tiles(nqk // 2, nqk)
    k_tiles(0, nqk // 2)
    conv_store(2, u)
    u = xbc_dot(3)
    k_tiles(nqk // 2, nqk)
    for j in range(D_ATTN // w):
        v_ref[:, j * w:(j + 1) * w] = _dot(ckvn, wuv_ref[:, j * w:(j + 1) * w]).astype(BF16)
    conv_store(3, u)


def _mix_in(x2d, l, p, ctab, stab, seq, tm):
    t = x2d.shape[0]
    hp = MLA_HEADS * LANES
    row = lambda w: pl.BlockSpec((tm, w), lambda i: (i, 0))
    halo = pl.BlockSpec((SUBLANES, D_MODEL),
                        lambda i: (jnp.maximum(i * (tm // SUBLANES) - 1, 0), 0))
    lay = lambda shape: _const_spec((None,) + shape, (l, 0, 0))
    return pl.pallas_call(
        functools.partial(_mix_in_kernel, tiles_per_seq=seq // tm),
        grid=(t // tm,),
        in_specs=[row(D_MODEL), halo, lay((1, D_MODEL)), lay((D_MODEL, _W_MAIN)),
                  lay((D_MODEL, LANES)), lay((SSM_CONV, CONV_CH)), lay((1, CONV_CH)),
                  lay((1, Q_LORA)), lay((Q_LORA, hp)), lay((1, KV_LORA)), lay((KV_LORA, hp)),
                  lay((KV_LORA, D_ATTN)), row(LANES), row(LANES)],
        out_specs=[row(D_SSM), row(CONV_CH), row(LANES), row(hp), row(hp), row(D_ATTN)],
        out_shape=[jax.ShapeDtypeStruct((t, D_SSM), BF16), jax.ShapeDtypeStruct((t, CONV_CH), BF16),
                   jax.ShapeDtypeStruct((t, LANES), F32), jax.ShapeDtypeStruct((t, hp), BF16),
                   jax.ShapeDtypeStruct((t, hp), BF16), jax.ShapeDtypeStruct((t, D_ATTN), BF16)],
        compiler_params=_params(("parallel",)),
        name="mix_in",
    )(x2d, x2d, p["norm_mix"], p["w_main"], p["w_small"], p["ssm_conv_w"], p["ssm_conv_b"],
      p["q_norm"], p["w_uq"], p["kv_norm"], p["w_uk"], p["w_uv"], ctab, stab)


def _split3(a):
    hi = a.astype(BF16)
    r = a - hi.astype(F32)
    mid = r.astype(BF16)
    lo = (r - mid.astype(F32)).astype(BF16)
    return hi, mid, lo


def _ssd_kernel(xbc_ref, z_ref, sm_ref, dtb_ref, alog_ref, dsk_ref, g_ref, o_ref, state_ref, *, tc):
    i = pl.program_id(1)

    @pl.when(i == 0)
    def _():
        state_ref[...] = jnp.zeros_like(state_ref)

    xc = xbc_ref[...].astype(F32)

    n = SSM_CHUNK
    ri = lax.broadcasted_iota(jnp.int32, (n, n), 0)
    ci = lax.broadcasted_iota(jnp.int32, (n, n), 1)
    causal = ri >= ci
    tri = jnp.where(causal, 1.0, 0.0).astype(BF16)
    lane = lax.broadcasted_iota(jnp.int32, (1, LANES), 1)
    dt_lanes = (lane >= DT_LO) & (lane < DT_HI)
    a_tile = jnp.where(dt_lanes, -jnp.exp(alog_ref[...]), 0.0)
    dtb = dtb_ref[...]
    bo = D_SSM
    co = D_SSM + SSM_GROUPS * SSM_STATE
    heads_per_group = SSM_HEADS // SSM_GROUPS

    for c in range(tc // n):
        rows = slice(c * n, (c + 1) * n)
        xs = xc[rows, 0:D_SSM]
        dtf = _softplus(sm_ref[rows, :] + dtb)
        ad = dtf * a_tile
        hi, mid, lo = _split3(ad)
        cs = _dot(tri, hi) + _dot(tri, mid) + _dot(tri, lo)
        cs_t = cs.T
        dt_t = dtf.T
        ys = []
        for g in range(SSM_GROUPS):
            bg = xc[rows, bo + g * SSM_STATE: bo + (g + 1) * SSM_STATE]
            cg = xc[rows, co + g * SSM_STATE: co + (g + 1) * SSM_STATE]
            bg_t = bg.T
            cg16 = cg.astype(BF16)
            cb = _dot_nt(cg16, bg.astype(BF16))
            for r in range(heads_per_group):
                hd = g * heads_per_group + r
                ln = DT_LO + hd
                col = cs[:, ln:ln + 1]
                row = cs_t[ln:ln + 1, :]
                dt_row = dt_t[ln:ln + 1, :]
                last = cs[n - 1:n, ln:ln + 1]
                lmat = jnp.exp(jnp.where(causal, col - row, -jnp.inf))
                mm = (cb * lmat * dt_row).astype(BF16)
                xh = xs[:, hd * SSM_HEAD_DIM:(hd + 1) * SSM_HEAD_DIM].astype(BF16)
                prev = state_ref[hd]
                y_h = _dot(mm, xh) + jnp.exp(col) * _dot(cg16, prev.astype(BF16))
                w_row = jnp.exp(last - row) * dt_row
                st_new = _dot((bg_t * w_row).astype(BF16), xh)
                state_ref[hd] = prev * jnp.exp(last) + st_new
                ys.append(y_h)
        y = jnp.concatenate(ys, axis=-1) + xs * dsk_ref[...]
        y = y * _silu(z_ref[rows, :].astype(F32))
        o_ref[rows, :] = _rms(y, g_ref[...]).astype(BF16)


def _ssd(xbc, z, sm, l, p, batch, seq, tc):
    ns = seq // tc
    main = lambda w: pl.BlockSpec((tc, w), lambda b, i: (b * ns + i, 0))
    lay = lambda shape: _const_spec((None,) + shape, (l, 0, 0))
    return pl.pallas_call(
        functools.partial(_ssd_kernel, tc=tc),
        grid=(batch, ns),
        in_specs=[main(CONV_CH), main(D_SSM), main(LANES), lay((1, LANES)), lay((1, LANES)),
                  lay((1, D_SSM)), lay((1, D_SSM))],
        out_specs=main(D_SSM),
        out_shape=jax.ShapeDtypeStruct((batch * seq, D_SSM), BF16),
        scratch_shapes=[pltpu.VMEM((SSM_HEADS, SSM_STATE, SSM_HEAD_DIM), F32)],
        compiler_params=_params(("arbitrary", "arbitrary")),
        name="ssd",
    )(xbc, z, sm, p["dt_bias"], p["a_log"], p["d_skip"], p["ssm_norm"])


def _attn_tile(q_ref, k_ref, v_ref, o_ref, *, c):
    tq = ATTN_SUB
    ri = lax.broadcasted_iota(jnp.int32, (tq, tq), 0)
    ci = lax.broadcasted_iota(jnp.int32, (tq, tq), 1)
    diag_mask = ri >= ci
    lane = lax.broadcasted_iota(jnp.int32, (tq, PV_WIDTH), 1)
    heads_per_pv = PV_WIDTH // V_DIM
    for sub in range(ATTN_STEP_SUBS):
        t = c * ATTN_STEP_SUBS + sub
        kv_len = (t + 1) * tq
        rows = slice(sub * tq, (sub + 1) * tq)
        out = None
        for hh in range(ATTN_HEADS):
            hs = slice(hh * LANES, (hh + 1) * LANES)
            r = hh % heads_per_pv
            vcols = slice((hh - r) * V_DIM, (hh - r) * V_DIM + PV_WIDTH)
            s = _dot_nt(q_ref[rows, hs], k_ref[0:kv_len, hs])
            diag = jnp.where(diag_mask, s[:, t * tq:], -jnp.inf)
            s = diag if t == 0 else jnp.concatenate([s[:, :t * tq], diag], axis=-1)
            m = jnp.max(s, axis=-1, keepdims=True)
            pr = jnp.exp2(s - m)
            l = jnp.sum(pr, axis=-1, keepdims=True)
            o = _dot(pr.astype(BF16), v_ref[0:kv_len, vcols]) / l
            out = o if r == 0 else jnp.where(lane >= r * V_DIM, o, out)
            if r == heads_per_pv - 1:
                o_ref[rows, vcols] = out.astype(BF16)


def _attn_kernel(q_ref, k_ref, v_ref, o_ref, *, nsteps):
    i = pl.program_id(2)
    for c in range(nsteps):
        pl.when(i == c)(functools.partial(_attn_tile, q_ref, k_ref, v_ref, o_ref, c=c))


def _mla_attn(q, k, v, batch, seq):
    tstep = ATTN_SUB * ATTN_STEP_SUBS
    nsteps = seq // tstep
    qw = ATTN_HEADS * LANES
    vw = ATTN_HEADS * V_DIM
    return pl.pallas_call(
        functools.partial(_attn_kernel, nsteps=nsteps),
        grid=(batch, MLA_HEADS // ATTN_HEADS, nsteps),
        in_specs=[pl.BlockSpec((tstep, qw), lambda b, j, i: (b * nsteps + i, j)),
                  pl.BlockSpec((seq, qw), lambda b, j, i: (b, j)),
                  pl.BlockSpec((seq, vw), lambda b, j, i: (b, j))],
        out_specs=pl.BlockSpec((tstep, vw), lambda b, j, i: (b * nsteps + i, j)),
        out_shape=jax.ShapeDtypeStruct((batch * seq, D_ATTN), BF16),
        compiler_params=_params(("parallel", "parallel", "arbitrary")),
        name="mla_attn",
    )(q, k, v)


def _mem_kv_kernel(m_ref, g_ref, wk_ref, wv_ref, k_ref, v_ref):
    h = _rms(m_ref[...], g_ref[...]).astype(BF16)
    k_ref[...] = _dot(h, wk_ref[...]).astype(BF16)
    v_ref[...] = _dot(h, wv_ref[...]).astype(BF16)


def _mem_kv(mem, g, wk, wv):
    depth = g.shape[0]
    batch, ml, _ = mem.shape
    lay = lambda shape: pl.BlockSpec((None,) + shape, lambda l, b: (l, 0, 0))
    out = pl.BlockSpec((None, None, ml, D_MODEL), lambda l, b: (l, b, 0, 0))
    return pl.pallas_call(
        _mem_kv_kernel,
        grid=(depth, batch),
        in_specs=[pl.BlockSpec((None, ml, D_MODEL), lambda l, b: (b, 0, 0)),
                  lay((1, D_MODEL)), lay((D_MODEL, D_MODEL)), lay((D_MODEL, D_MODEL))],
        out_specs=[out, out],
        out_shape=[jax.ShapeDtypeStruct((depth, batch, ml, D_MODEL), BF16)] * 2,
        compiler_params=_params(("arbitrary", "arbitrary")),
        name="mem_kv",
    )(mem, g, wk, wv)


def _mix_out_mem_kernel(x_ref, ys_ref, ya_ref, ag_ref, wo_ref, mg_ref, wq_ref, km_ref, vm_ref,
                        wmo_ref, o_ref):
    ya = _rms(ya_ref[...].astype(F32), ag_ref[...]).astype(BF16)
    x1 = x_ref[...] + _dot(ys_ref[...], wo_ref[0:D_SSM, :]) + _dot(ya, wo_ref[D_SSM:, :])
    hq = _rms(x1, mg_ref[...]).astype(BF16)
    qm = (_dot(hq, wq_ref[...]) * (MEM_HEAD_DIM ** -0.5)).astype(BF16)
    outs = []
    for hd in range(MEM_HEADS):
        hs = slice(hd * MEM_HEAD_DIM, (hd + 1) * MEM_HEAD_DIM)
        s = _dot_nt(qm[:, hs], km_ref[:, hs])
        m = jnp.max(s, axis=-1, keepdims=True)
        pr = jnp.exp(s - m)
        l = jnp.sum(pr, axis=-1, keepdims=True)
        outs.append(_dot(pr.astype(BF16), vm_ref[:, hs]) / l)
    o = jnp.concatenate(outs, axis=-1).astype(BF16)
    o_ref[...] = x1 + _dot(o, wmo_ref[...])


def _mix_out_mem(x2d, ys, ya, kmem, vmem, l, p, batch, seq, tm):
    ns = seq // tm
    ml = kmem.shape[2]
    row = lambda w: pl.BlockSpec((tm, w), lambda b, i: (b * ns + i, 0))
    lay = lambda shape: _const_spec((None,) + shape, (l, 0, 0))
    memspec = pl.BlockSpec((None, None, ml, D_MODEL), lambda b, i: (l, b, 0, 0))
    return pl.pallas_call(
        _mix_out_mem_kernel,
        grid=(batch, ns),
        in_specs=[row(D_MODEL), row(D_SSM), row(D_ATTN), lay((1, D_ATTN)),
                  lay((D_SSM + D_ATTN, D_MODEL)), lay((1, D_MODEL)), lay((D_MODEL, D_MODEL)),
                  memspec, memspec, lay((D_MODEL, D_MODEL))],
        out_specs=row(D_MODEL),
        out_shape=jax.ShapeDtypeStruct(x2d.shape, F32),
        compiler_params=_params(("parallel", "parallel")),
        name="mix_out_mem",
    )(x2d, ys, ya, p["attn_out_norm"], p["w_out"], p["norm_mem_q"], p["w_mq"], kmem, vmem,
      p["w_mo"])


def _ffn_kernel(x_ref, xh_ref, g_ref, wup_ref, cw_ref, cb_ref, wdn_ref, fg_ref, o_ref, act_ref, *,
                final):
    i = pl.program_id(1)
    x = x_ref[...]
    xh = jnp.where(i > 0, xh_ref[...], 0.0)
    h = _rms(jnp.concatenate([xh, x], axis=0), g_ref[...]).astype(BF16)

    def conv(u, cols):
        w = cw_ref[:, cols]
        out = cb_ref[:, cols] + w[FFN_CONV - 1:FFN_CONV, :] * u[SUBLANES:, :]
        for j in range(FFN_CONV - 1):
            shift = FFN_CONV - 1 - j
            out = out + w[j:j + 1, :] * pltpu.roll(u, shift, 0)[SUBLANES:, :]
        return out

    nchunks = D_FF // FFN_CHUNK
    gcols = lambda c: slice(c * FFN_CHUNK, (c + 1) * FFN_CHUNK)
    vcols = lambda c: slice(D_FF + c * FFN_CHUNK, D_FF + (c + 1) * FFN_CHUNK)
    up = lambda c: (_dot(h, wup_ref[:, gcols(c)]), _dot(h, wup_ref[:, vcols(c)]))

    nxt = up(0)
    for c in range(nchunks):
        ug, uv = nxt
        if c + 1 < nchunks:
            nxt = up(c + 1)
        act_ref[:, gcols(c)] = (_silu(conv(ug, gcols(c))) * conv(uv, vcols(c))).astype(BF16)
    acc = x + _dot(act_ref[...], wdn_ref[...])
    if final:
        acc = _rms(acc, fg_ref[...])
    o_ref[...] = acc


def _ffn(x2d, l, p, final_norm, batch, seq, tm, final):
    ns = seq // tm
    halo_blocks_per_seq = seq // SUBLANES
    row = pl.BlockSpec((tm, D_MODEL), lambda b, i: (b * ns + i, 0))
    halo = pl.BlockSpec(
        (SUBLANES, D_MODEL),
        lambda b, i: (jnp.maximum(b * halo_blocks_per_seq + i * (tm // SUBLANES) - 1, 0), 0))
    lay = lambda shape: _const_spec((None,) + shape, (l, 0, 0))
    return pl.pallas_call(
        functools.partial(_ffn_kernel, final=final),
        grid=(batch, ns),
        in_specs=[row, halo, lay((1, D_MODEL)), lay((D_MODEL, 2 * D_FF)), lay((FFN_CONV, 2 * D_FF)),
                  lay((1, 2 * D_FF)), lay((D_FF, D_MODEL)), _const_spec((1, D_MODEL), (0, 0))],
        out_specs=row,
        out_shape=jax.ShapeDtypeStruct(x2d.shape, F32),
        compiler_params=_params(("parallel", "parallel")),
        scratch_shapes=[pltpu.VMEM((tm, D_FF), BF16)],
        name="ffn_final" if final else "ffn",
    )(x2d, x2d, p["norm_ffn"], p["w_up"], p["ffn_conv_w"], p["ffn_conv_b"], p["w_down"], final_norm)


def _prep_params(norm_mix, w_in, ssm_conv_w, ssm_conv_b, dt_bias, a_log, d_skip, ssm_norm, q_norm,
                 w_uq, kv_norm, w_ukv, attn_out_norm, w_out, norm_mem_q, w_mq, w_mo, norm_ffn, w_up,
                 ffn_conv_w, ffn_conv_b, w_down):
    depth = w_in.shape[0]
    o1 = D_SSM
    o2 = o1 + CONV_CH
    o3 = o2 + SSM_HEADS
    o4 = o3 + Q_LORA
    o5 = o4 + KV_LORA
    vec = lambda a: a.reshape(depth, 1, -1)
    w_main = jnp.concatenate([w_in[..., :o2], w_in[..., o3:o5]], axis=-1).astype(BF16)
    zeros = lambda w: jnp.zeros((depth, D_MODEL, w), w_in.dtype)
    w_small = jnp.concatenate([zeros(ROPE_LO), w_in[..., o5:], w_in[..., o2:o3],
                               zeros(LANES - DT_HI)], axis=-1).astype(BF16)
    uq = w_uq.reshape(depth, Q_LORA, MLA_HEADS, QK_NOPE + QK_ROPE)
    uq = jnp.pad(uq, ((0, 0), (0, 0), (0, 0), (0, LANES - QK_NOPE - QK_ROPE)))
    ukv = w_ukv.reshape(depth, KV_LORA, MLA_HEADS, QK_NOPE + V_DIM)
    uk = jnp.pad(ukv[..., :QK_NOPE], ((0, 0), (0, 0), (0, 0), (0, LANES - QK_NOPE)))
    uv = ukv[..., QK_NOPE:]
    lane_pad = lambda a: jnp.pad(a, ((0, 0), (DT_LO, LANES - DT_HI))).reshape(depth, 1, LANES)
    return {
        "norm_mix": vec(norm_mix), "w_main": w_main, "w_small": w_small,
        "q_norm": vec(q_norm), "w_uq": uq.reshape(depth, Q_LORA, MLA_HEADS * LANES).astype(BF16),
        "kv_norm": vec(kv_norm), "w_uk": uk.reshape(depth, KV_LORA, MLA_HEADS * LANES).astype(BF16),
        "w_uv": uv.reshape(depth, KV_LORA, D_ATTN).astype(BF16),
        "ssm_conv_w": ssm_conv_w, "ssm_conv_b": vec(ssm_conv_b),
        "dt_bias": lane_pad(dt_bias), "a_log": lane_pad(a_log),
        "d_skip": jnp.repeat(d_skip, SSM_HEAD_DIM, axis=-1).reshape(depth, 1, D_SSM),
        "ssm_norm": vec(ssm_norm), "attn_out_norm": vec(attn_out_norm),
        "w_out": w_out.astype(BF16), "norm_mem_q": vec(norm_mem_q), "w_mq": w_mq.astype(BF16),
        "w_mo": w_mo.astype(BF16), "norm_ffn": vec(norm_ffn), "w_up": w_up.astype(BF16),
        "ffn_conv_w": ffn_conv_w, "ffn_conv_b": vec(ffn_conv_b), "w_down": w_down.astype(BF16),
    }


def kernel(x, mem, positions, norm_mix, w_in, ssm_conv_w, ssm_conv_b, dt_bias, a_log, d_skip,
           ssm_norm, q_norm, w_uq, kv_norm, w_ukv, attn_out_norm, w_out, norm_mem_q, norm_mem_kv,
           w_mq, w_mk, w_mv, w_mo, norm_ffn, w_up, ffn_conv_w, ffn_conv_b, w_down, final_norm):
    batch, seq, _ = x.shape
    depth = w_in.shape[0]
    p = _prep_params(norm_mix, w_in, ssm_conv_w, ssm_conv_b, dt_bias, a_log, d_skip, ssm_norm,
                     q_norm, w_uq, kv_norm, w_ukv, attn_out_norm, w_out, norm_mem_q, w_mq, w_mo,
                     norm_ffn, w_up, ffn_conv_w, ffn_conv_b, w_down)
    ctab, stab = _rope_tables(positions)
    kmem, vmem = _mem_kv(mem, norm_mem_kv.reshape(depth, 1, D_MODEL), w_mk.astype(BF16),
                         w_mv.astype(BF16))
    fnorm = final_norm.reshape(1, D_MODEL)
    tm_in = min(512, seq)
    tm = min(512, seq)
    tc = min(512, seq)
    x2d = x.reshape(batch * seq, D_MODEL)
    for l in range(depth):
        z, xbc, sm, q, k, v = _mix_in(x2d, l, p, ctab, stab, seq, tm_in)
        ys = _ssd(xbc, z, sm, l, p, batch, seq, tc)
        ya = _mla_attn(q, k, v, batch, seq)
        x2d = _mix_out_mem(x2d, ys, ya, kmem, vmem, l, p, batch, seq, tm)
        x2d = _ffn(x2d, l, p, fnorm, batch, seq, tm, final=(l == depth - 1))
    return x2d.reshape(batch, seq, D_MODEL)
```

```python
import functools
import math

import jax
import jax.numpy as jnp
from jax import lax
from jax.experimental import pallas as pl
from jax.experimental.pallas import tpu as pltpu

F32 = jnp.float32
BF16 = jnp.bfloat16

D_MODEL = 1024
EPS = 1e-6
SSM_HEADS = 16
SSM_HEAD_DIM = 64
D_SSM = SSM_HEADS * SSM_HEAD_DIM
SSM_GROUPS = 4
SSM_STATE = 128
SSM_CONV = 4
SSM_CHUNK = 128
CONV_CH = D_SSM + 2 * SSM_GROUPS * SSM_STATE
MLA_HEADS = 16
QK_NOPE = 64
QK_ROPE = 32
V_DIM = 64
Q_LORA = 384
KV_LORA = 256
D_ATTN = MLA_HEADS * V_DIM
ROPE_THETA = 10000.0
MEM_HEADS = 4
MEM_HEAD_DIM = D_MODEL // MEM_HEADS
D_FF = 2816
FFN_CONV = 3

LANES = 128
SUBLANES = 8
MXU_TILE = 256
VMEM_LIMIT = 56 * 1024 * 1024

ROPE_LO = QK_NOPE
ROPE_MID = QK_NOPE + QK_ROPE // 2
ROPE_HI = QK_NOPE + QK_ROPE
DT_LO = ROPE_HI
DT_HI = ROPE_HI + SSM_HEADS

FFN_CHUNK = 256
CONV_CHUNK = 256
assert SSM_CONV == 4 and CONV_CH % CONV_CHUNK == 0
ATTN_SUB = 256
ATTN_STEP_SUBS = 2
ATTN_HEADS = 8
PV_WIDTH = 256


def _dot(a, b):
    return jnp.dot(a, b, preferred_element_type=F32)


def _dot_nt(a, b):
    return lax.dot_general(a, b, (((1,), (1,)), ((), ())), preferred_element_type=F32)


def _rms(xf, g):
    var = jnp.mean(xf * xf, axis=-1, keepdims=True)
    return xf * lax.rsqrt(var + EPS) * g


def _silu(x):
    return x * (1.0 / (1.0 + jnp.exp(-x)))


def _softplus(x):
    return jnp.maximum(x, 0.0) + jnp.log1p(jnp.exp(-jnp.abs(x)))


def _rope128(t, c, s, lane):
    rot = jnp.where(lane < ROPE_MID, pltpu.roll(t, LANES - QK_ROPE // 2, 1),
                    pltpu.roll(t, QK_ROPE // 2, 1))
    return t * c + rot * s


def _params(sem):
    return pltpu.CompilerParams(dimension_semantics=sem, vmem_limit_bytes=VMEM_LIMIT)


def _const_spec(shape, index):
    return pl.BlockSpec(shape, lambda *_: index, pipeline_mode=pl.Buffered(1))


def _rope_kernel(pos_ref, inv_ref, cos_ref, sin_ref):
    ang = pos_ref[...] * inv_ref[...]
    cos_ref[...] = jnp.cos(ang)
    sin_ref[...] = jnp.sin(ang)


def _rope_tables(positions):
    t = positions.size
    half = QK_ROPE // 2
    per_row = LANES // half
    inv_freq = 1.0 / (ROPE_THETA ** (jnp.arange(0, QK_ROPE, 2, dtype=F32) / QK_ROPE))
    posd = jnp.repeat(positions.reshape(t).astype(F32), half).reshape(t // per_row, LANES)
    invd = jnp.tile(inv_freq, per_row).reshape(1, LANES)
    rows = t // per_row
    cosd, sind = pl.pallas_call(
        _rope_kernel,
        out_shape=(jax.ShapeDtypeStruct((rows, LANES), F32),) * 2,
        name="rope_tables",
    )(posd, invd)
    cos = cosd.reshape(t, half)
    sin = sind.reshape(t, half)
    ctab = jnp.concatenate([jnp.ones((t, QK_NOPE), F32), cos, cos,
                            jnp.zeros((t, LANES - ROPE_HI), F32)], axis=-1)
    stab = jnp.concatenate([jnp.zeros((t, QK_NOPE), F32), -sin, sin,
                            jnp.zeros((t, LANES - ROPE_HI), F32)], axis=-1)
    return ctab, stab


_O_Z = 0
_O_XBC = D_SSM
_W_MAIN = _O_XBC + CONV_CH
_L_CQ = 0
_L_CKV = Q_LORA
_L_SM = Q_LORA + KV_LORA
_W_LAT = _L_SM + LANES
_QK_SCALE = (QK_NOPE + QK_ROPE) ** -0.5 * math.log2(math.e)


def _mix_in_kernel(x_ref, xh_ref, g_ref, wm_ref, wl_ref, cw_ref, cb_ref, qg_ref, wuq_ref, kg_ref,
                   wuk_ref, wuv_ref, c_ref, s_ref, z_ref, xbc_ref, sm_ref, q_ref, k_ref, v_ref,
                   *, tiles_per_seq):
    first = pl.program_id(0) % tiles_per_seq == 0
    xh = jnp.where(first, 0.0, xh_ref[...])
    h_full = _rms(jnp.concatenate([xh, x_ref[...]], axis=0), g_ref[...]).astype(BF16)
    h = h_full[SUBLANES:, :]

    def xbc_dot(cc):
        lo = _O_XBC + cc * CONV_CHUNK
        return _dot(h_full, wm_ref[:, lo:lo + CONV_CHUNK])

    def conv_store(cc, u):
        cols = slice(cc * CONV_CHUNK, (cc + 1) * CONV_CHUNK)
        cw = cw_ref[:, cols]
        u1 = pltpu.roll(u, 1, 0)
        near = cw[3:4, :] * u + cw[2:3, :] * u1
        far = cw[1:2, :] * u + cw[0:1, :] * u1
        acc = cb_ref[:, cols] + near[SUBLANES:, :] + pltpu.roll(far, 2, 0)[SUBLANES:, :]
        xbc_ref[:, cols] = _silu(acc).astype(BF16)

    w = MXU_TILE
    heads_per_tile = w // LANES

    u = xbc_dot(0)
    cq = _dot(h, wl_ref[:, _L_CQ:_L_CKV])
    ckv = _dot(h, wl_ref[:, _L_CKV:_L_SM])
    sm = _dot(h, wl_ref[:, _L_SM:_W_LAT])
    sm_ref[...] = sm
    c = c_ref[...]
    s = s_ref[...]
    lane = lax.broadcasted_iota(jnp.int32, c.shape, 1)
    cs = c * _QK_SCALE
    ss = s * _QK_SCALE
    cqn = _rms(cq, qg_ref[...]).astype(BF16)
    ckvn = _rms(ckv, kg_ref[...]).astype(BF16)
    kpe = _rope128(sm, c, s, lane)
    kpe = jnp.where((lane >= ROPE_LO) & (lane < ROPE_HI), kpe, 0.0)

    def z_tile(j):
        z_ref[:, j * w:(j + 1) * w] = _dot(h, wm_ref[:, _O_Z + j * w:_O_Z + (j + 1) * w]).astype(BF16)

    def q_tile(j):
        qq = _dot(cqn, wuq_ref[:, j * w:(j + 1) * w])
        for r in range(heads_per_tile):
            sl = slice(j * w + r * LANES, j * w + (r + 1) * LANES)
            q_ref[:, sl] = _rope128(qq[:, r * LANES:(r + 1) * LANES], cs, ss, lane).astype(BF16)

    def k_tile(j):
        kk = _dot(ckvn, wuk_ref[:, j * w:(j + 1) * w])
        for r in range(heads_per_tile):
            sl = slice(j * w + r * LANES, j * w + (r + 1) * LANES)
            k_ref[:, sl] = (kk[:, r * LANES:(r + 1) * LANES] + kpe).astype(BF16)

    def v_tile(j):
        v_ref[:, j * w:(j + 1) * w] = _dot(ckvn, wuv_ref[:, j * w:(j + 1) * w]).astype(BF16)

    nqk = MLA_HEADS // heads_per_tile
    tasks = ([functools.partial(z_tile, j) for j in range(D_SSM // w)]
             + [functools.partial(q_tile, j) for j in range(nqk)]
             + [functools.partial(k_tile, j) for j in range(nqk)]
             + [functools.partial(v_tile, j) for j in range(D_ATTN // w)])
    nconv = CONV_CH // CONV_CHUNK
    per = -(-len(tasks) // nconv)
    for cc in range(nconv):
        nxt = xbc_dot(cc + 1) if cc + 1 < nconv else None
        conv_store(cc, u)
        for task in tasks[cc * per:(cc + 1) * per]:
            task()
        u = nxt


def _mix_in(x2d, l, p, ctab, stab, seq, tm):
    t = x2d.shape[0]
    hp = MLA_HEADS * LANES
    row = lambda w: pl.BlockSpec((tm, w), lambda i: (i, 0))
    halo = pl.BlockSpec((SUBLANES, D_MODEL),
                        lambda i: (jnp.maximum(i * (tm // SUBLANES) - 1, 0), 0))
    lay = lambda shape: _const_spec((None,) + shape, (l, 0, 0))
    return pl.pallas_call(
        functools.partial(_mix_in_kernel, tiles_per_seq=seq // tm),
        grid=(t // tm,),
        in_specs=[row(D_MODEL), halo, lay((1, D_MODEL)), lay((D_MODEL, _W_MAIN)),
                  lay((D_MODEL, _W_LAT)), lay((SSM_CONV, CONV_CH)), lay((1, CONV_CH)),
                  lay((1, Q_LORA)), lay((Q_LORA, hp)), lay((1, KV_LORA)), lay((KV_LORA, hp)),
                  lay((KV_LORA, D_ATTN)), row(LANES), row(LANES)],
        out_specs=[row(D_SSM), row(CONV_CH), row(LANES), row(hp), row(hp), row(D_ATTN)],
        out_shape=[jax.ShapeDtypeStruct((t, D_SSM), BF16), jax.ShapeDtypeStruct((t, CONV_CH), BF16),
                   jax.ShapeDtypeStruct((t, LANES), F32), jax.ShapeDtypeStruct((t, hp), BF16),
                   jax.ShapeDtypeStruct((t, hp), BF16), jax.ShapeDtypeStruct((t, D_ATTN), BF16)],
        compiler_params=_params(("parallel",)),
        name="mix_in",
    )(x2d, x2d, p["norm_mix"], p["w_main"], p["w_lat"], p["ssm_conv_w"], p["ssm_conv_b"],
      p["q_norm"], p["w_uq"], p["kv_norm"], p["w_uk"], p["w_uv"], ctab, stab)


def _split3(a):
    hi = a.astype(BF16)
    r = a - hi.astype(F32)
    mid = r.astype(BF16)
    lo = (r - mid.astype(F32)).astype(BF16)
    return hi, mid, lo


def _ssd_kernel(xbc_ref, z_ref, sm_ref, dtb_ref, alog_ref, dsk_ref, g_ref, o_ref, state_ref, *, tc):
    i = pl.program_id(1)

    @pl.when(i == 0)
    def _():
        state_ref[...] = jnp.zeros_like(state_ref)

    xc = xbc_ref[...].astype(F32)

    n = SSM_CHUNK
    ri = lax.broadcasted_iota(jnp.int32, (n, n), 0)
    ci = lax.broadcasted_iota(jnp.int32, (n, n), 1)
    causal = ri >= ci
    tri = jnp.where(causal, 1.0, 0.0).astype(BF16)
    lane = lax.broadcasted_iota(jnp.int32, (1, LANES), 1)
    dt_lanes = (lane >= DT_LO) & (lane < DT_HI)
    a_tile = jnp.where(dt_lanes, -jnp.exp(alog_ref[...]), 0.0)
    dtb = dtb_ref[...]
    bo = D_SSM
    co = D_SSM + SSM_GROUPS * SSM_STATE
    heads_per_group = SSM_HEADS // SSM_GROUPS

    for c in range(tc // n):
        rows = slice(c * n, (c + 1) * n)
        xs = xc[rows, 0:D_SSM]
        dtf = _softplus(sm_ref[rows, :] + dtb)
        ad = dtf * a_tile
        hi, mid, lo = _split3(ad)
        cs = _dot(tri, hi) + _dot(tri, mid) + _dot(tri, lo)
        cs_t = cs.T
        dt_t = dtf.T
        ys = []
        for g in range(SSM_GROUPS):
            bg = xc[rows, bo + g * SSM_STATE: bo + (g + 1) * SSM_STATE]
            cg = xc[rows, co + g * SSM_STATE: co + (g + 1) * SSM_STATE]
            bg_t = bg.T
            cg16 = cg.astype(BF16)
            cb = _dot_nt(cg16, bg.astype(BF16))
            for r in range(heads_per_group):
                hd = g * heads_per_group + r
                ln = DT_LO + hd
                col = cs[:, ln:ln + 1]
                row = cs_t[ln:ln + 1, :]
                dt_row = dt_t[ln:ln + 1, :]
                last = cs[n - 1:n, ln:ln + 1]
                lmat = jnp.exp(jnp.where(causal, col - row, -jnp.inf))
                mm = (cb * lmat * dt_row).astype(BF16)
                xh = xs[:, hd * SSM_HEAD_DIM:(hd + 1) * SSM_HEAD_DIM].astype(BF16)
                prev = state_ref[hd]
                y_h = _dot(mm, xh) + jnp.exp(col) * _dot(cg16, prev.astype(BF16))
                w_row = jnp.exp(last - row) * dt_row
                st_new = _dot((bg_t * w_row).astype(BF16), xh)
                state_ref[hd] = prev * jnp.exp(last) + st_new
                ys.append(y_h)
        y = jnp.concatenate(ys, axis=-1) + xs * dsk_ref[...]
        y = y * _silu(z_ref[rows, :].astype(F32))
        o_ref[rows, :] = _rms(y, g_ref[...]).astype(BF16)


def _ssd(xbc, z, sm, l, p, batch, seq, tc):
    ns = seq // tc
    main = lambda w: pl.BlockSpec((tc, w), lambda b, i: (b * ns + i, 0))
    lay = lambda shape: _const_spec((None,) + shape, (l, 0, 0))
    return pl.pallas_call(
        functools.partial(_ssd_kernel, tc=tc),
        grid=(batch, ns),
        in_specs=[main(CONV_CH), main(D_SSM), main(LANES), lay((1, LANES)), lay((1, LANES)),
                  lay((1, D_SSM)), lay((1, D_SSM))],
        out_specs=main(D_SSM),
        out_shape=jax.ShapeDtypeStruct((batch * seq, D_SSM), BF16),
        scratch_shapes=[pltpu.VMEM((SSM_HEADS, SSM_STATE, SSM_HEAD_DIM), F32)],
        compiler_params=_params(("arbitrary", "arbitrary")),
        name="ssd",
    )(xbc, z, sm, p["dt_bias"], p["a_log"], p["d_skip"], p["ssm_norm"])


def _attn_tile(q_ref, k_ref, v_ref, o_ref, *, c):
    tq = ATTN_SUB
    ri = lax.broadcasted_iota(jnp.int32, (tq, tq), 0)
    ci = lax.broadcasted_iota(jnp.int32, (tq, tq), 1)
    diag_mask = ri >= ci
    lane = lax.broadcasted_iota(jnp.int32, (tq, PV_WIDTH), 1)
    heads_per_pv = PV_WIDTH // V_DIM
    for sub in range(ATTN_STEP_SUBS):
        t = c * ATTN_STEP_SUBS + sub
        kv_len = (t + 1) * tq
        rows = slice(sub * tq, (sub + 1) * tq)
        out = None
        for hh in range(ATTN_HEADS):
            hs = slice(hh * LANES, (hh + 1) * LANES)
            r = hh % heads_per_pv
            vcols = slice((hh - r) * V_DIM, (hh - r) * V_DIM + PV_WIDTH)
            s = _dot_nt(q_ref[rows, hs], k_ref[0:kv_len, hs])
            diag = jnp.where(diag_mask, s[:, t * tq:], -jnp.inf)
            s = diag if t == 0 else jnp.concatenate([s[:, :t * tq], diag], axis=-1)
            m = jnp.max(s, axis=-1, keepdims=True)
            pr = jnp.exp2(s - m)
            l = jnp.sum(pr, axis=-1, keepdims=True)
            o = _dot(pr.astype(BF16), v_ref[0:kv_len, vcols]) / l
            out = o if r == 0 else jnp.where(lane >= r * V_DIM, o, out)
            if r == heads_per_pv - 1:
                o_ref[rows, vcols] = out.astype(BF16)


def _attn_kernel(q_ref, k_ref, v_ref, o_ref, *, nsteps):
    i = pl.program_id(2)
    for c in range(nsteps):
        pl.when(i == c)(functools.partial(_attn_tile, q_ref, k_ref, v_ref, o_ref, c=c))


def _mla_attn(q, k, v, batch, seq):
    tstep = ATTN_SUB * ATTN_STEP_SUBS
    nsteps = seq // tstep
    qw = ATTN_HEADS * LANES
    vw = ATTN_HEADS * V_DIM
    return pl.pallas_call(
        functools.partial(_attn_kernel, nsteps=nsteps),
        grid=(batch, MLA_HEADS // ATTN_HEADS, nsteps),
        in_specs=[pl.BlockSpec((tstep, qw), lambda b, j, i: (b * nsteps + i, j)),
                  pl.BlockSpec((seq, qw), lambda b, j, i: (b, j)),
                  pl.BlockSpec((seq, vw), lambda b, j, i: (b, j))],
        out_specs=pl.BlockSpec((tstep, vw), lambda b, j, i: (b * nsteps + i, j)),
        out_shape=jax.ShapeDtypeStruct((batch * seq, D_ATTN), BF16),
        compiler_params=_params(("parallel", "parallel", "arbitrary")),
        name="mla_attn",
    )(q, k, v)


def _mem_kv_kernel(m_ref, g_ref, wk_ref, wv_ref, k_ref, v_ref):
    h = _rms(m_ref[...], g_ref[...]).astype(BF16)
    k_ref[...] = _dot(h, wk_ref[...]).astype(BF16)
    v_ref[...] = _dot(h, wv_ref[...]).astype(BF16)


def _mem_kv(mem, g, wk, wv):
    depth = g.shape[0]
    batch, ml, _ = mem.shape
    rows = batch * ml
    lay = lambda shape: pl.BlockSpec((None,) + shape, lambda l: (l, 0, 0))
    return pl.pallas_call(
        _mem_kv_kernel,
        grid=(depth,),
        in_specs=[pl.BlockSpec((rows, D_MODEL), lambda l: (0, 0)),
                  lay((1, D_MODEL)), lay((D_MODEL, D_MODEL)), lay((D_MODEL, D_MODEL))],
        out_specs=[lay((rows, D_MODEL)), lay((rows, D_MODEL))],
        out_shape=[jax.ShapeDtypeStruct((depth, rows, D_MODEL), BF16)] * 2,
        compiler_params=_params(("arbitrary",)),
        name="mem_kv",
    )(mem.reshape(rows, D_MODEL), g, wk, wv)


def _mix_out_mem_kernel(x_ref, ys_ref, ya_ref, ag_ref, wo_ref, mg_ref, wq_ref, km_ref, vm_ref,
                        wmo_ref, o_ref):
    ya = _rms(ya_ref[...].astype(F32), ag_ref[...]).astype(BF16)
    x1 = x_ref[...] + _dot(ys_ref[...], wo_ref[0:D_SSM, :]) + _dot(ya, wo_ref[D_SSM:, :])
    hq = _rms(x1, mg_ref[...]).astype(BF16)
    qm = (_dot(hq, wq_ref[...]) * (MEM_HEAD_DIM ** -0.5)).astype(BF16)
    outs = []
    for hd in range(MEM_HEADS):
        hs = slice(hd * MEM_HEAD_DIM, (hd + 1) * MEM_HEAD_DIM)
        s = _dot_nt(qm[:, hs], km_ref[:, hs])
        m = jnp.max(s, axis=-1, keepdims=True)
        pr = jnp.exp(s - m)
        l = jnp.sum(pr, axis=-1, keepdims=True)
        outs.append(_dot(pr.astype(BF16), vm_ref[:, hs]) / l)
    o = jnp.concatenate(outs, axis=-1).astype(BF16)
    o_ref[...] = x1 + _dot(o, wmo_ref[...])


def _mix_out_mem(x2d, ys, ya, kmem, vmem, l, p, batch, seq, tm):
    ns = seq // tm
    ml = kmem.shape[1] // batch
    row = lambda w: pl.BlockSpec((tm, w), lambda b, i: (b * ns + i, 0))
    lay = lambda shape: _const_spec((None,) + shape, (l, 0, 0))
    memspec = pl.BlockSpec((None, ml, D_MODEL), lambda b, i: (l, b, 0))
    return pl.pallas_call(
        _mix_out_mem_kernel,
        grid=(batch, ns),
        in_specs=[row(D_MODEL), row(D_SSM), row(D_ATTN), lay((1, D_ATTN)),
                  lay((D_SSM + D_ATTN, D_MODEL)), lay((1, D_MODEL)), lay((D_MODEL, D_MODEL)),
                  memspec, memspec, lay((D_MODEL, D_MODEL))],
        out_specs=row(D_MODEL),
        out_shape=jax.ShapeDtypeStruct(x2d.shape, F32),
        compiler_params=_params(("parallel", "parallel")),
        name="mix_out_mem",
    )(x2d, ys, ya, p["attn_out_norm"], p["w_out"], p["norm_mem_q"], p["w_mq"], kmem, vmem,
      p["w_mo"])


def _ffn_kernel(x_ref, xh_ref, g_ref, wup_ref, cw_ref, cb_ref, wdn_ref, fg_ref, o_ref, act_ref, *,
                final):
    i = pl.program_id(1)
    x = x_ref[...]
    xh = jnp.where(i > 0, xh_ref[...], 0.0)
    h = _rms(jnp.concatenate([xh, x], axis=0), g_ref[...]).astype(BF16)

    def conv(u, cols):
        w = cw_ref[:, cols]
        out = cb_ref[:, cols] + w[FFN_CONV - 1:FFN_CONV, :] * u[SUBLANES:, :]
        for j in range(FFN_CONV - 1):
            shift = FFN_CONV - 1 - j
            out = out + w[j:j + 1, :] * pltpu.roll(u, shift, 0)[SUBLANES:, :]
        return out

    nchunks = D_FF // FFN_CHUNK
    gcols = lambda c: slice(c * FFN_CHUNK, (c + 1) * FFN_CHUNK)
    vcols = lambda c: slice(D_FF + c * FFN_CHUNK, D_FF + (c + 1) * FFN_CHUNK)
    up = lambda c: (_dot(h, wup_ref[:, gcols(c)]), _dot(h, wup_ref[:, vcols(c)]))

    nxt = up(0)
    for c in range(nchunks):
        ug, uv = nxt
        if c + 1 < nchunks:
            nxt = up(c + 1)
        act_ref[:, gcols(c)] = (_silu(conv(ug, gcols(c))) * conv(uv, vcols(c))).astype(BF16)
    acc = x + _dot(act_ref[...], wdn_ref[...])
    if final:
        acc = _rms(acc, fg_ref[...])
    o_ref[...] = acc


def _ffn(x2d, l, p, final_norm, batch, seq, tm, final):
    ns = seq // tm
    halo_blocks_per_seq = seq // SUBLANES
    row = pl.BlockSpec((tm, D_MODEL), lambda b, i: (b * ns + i, 0))
    halo = pl.BlockSpec(
        (SUBLANES, D_MODEL),
        lambda b, i: (jnp.maximum(b * halo_blocks_per_seq + i * (tm // SUBLANES) - 1, 0), 0))
    lay = lambda shape: _const_spec((None,) + shape, (l, 0, 0))
    return pl.pallas_call(
        functools.partial(_ffn_kernel, final=final),
        grid=(batch, ns),
        in_specs=[row, halo, lay((1, D_MODEL)), lay((D_MODEL, 2 * D_FF)), lay((FFN_CONV, 2 * D_FF)),
                  lay((1, 2 * D_FF)), lay((D_FF, D_MODEL)), _const_spec((1, D_MODEL), (0, 0))],
        out_specs=row,
        out_shape=jax.ShapeDtypeStruct(x2d.shape, F32),
        compiler_params=_params(("parallel", "parallel")),
        scratch_shapes=[pltpu.VMEM((tm, D_FF), BF16)],
        name="ffn_final" if final else "ffn",
    )(x2d, x2d, p["norm_ffn"], p["w_up"], p["ffn_conv_w"], p["ffn_conv_b"], p["w_down"], final_norm)


def _prep_params(norm_mix, w_in, ssm_conv_w, ssm_conv_b, dt_bias, a_log, d_skip, ssm_norm, q_norm,
                 w_uq, kv_norm, w_ukv, attn_out_norm, w_out, norm_mem_q, w_mq, w_mo, norm_ffn, w_up,
                 ffn_conv_w, ffn_conv_b, w_down):
    depth = w_in.shape[0]
    o1 = D_SSM
    o2 = o1 + CONV_CH
    o3 = o2 + SSM_HEADS
    o4 = o3 + Q_LORA
    o5 = o4 + KV_LORA
    vec = lambda a: a.reshape(depth, 1, -1)
    w_main = w_in[..., :o2].astype(BF16)
    zeros = lambda w: jnp.zeros((depth, D_MODEL, w), w_in.dtype)
    w_lat = jnp.concatenate([w_in[..., o3:o5], zeros(ROPE_LO), w_in[..., o5:], w_in[..., o2:o3],
                             zeros(LANES - DT_HI)], axis=-1).astype(BF16)
    uq = w_uq.reshape(depth, Q_LORA, MLA_HEADS, QK_NOPE + QK_ROPE)
    uq = jnp.pad(uq, ((0, 0), (0, 0), (0, 0), (0, LANES - QK_NOPE - QK_ROPE)))
    ukv = w_ukv.reshape(depth, KV_LORA, MLA_HEADS, QK_NOPE + V_DIM)
    uk = jnp.pad(ukv[..., :QK_NOPE], ((0, 0), (0, 0), (0, 0), (0, LANES - QK_NOPE)))
    uv = ukv[..., QK_NOPE:]
    lane_pad = lambda a: jnp.pad(a, ((0, 0), (DT_LO, LANES - DT_HI))).reshape(depth, 1, LANES)
    return {
        "norm_mix": vec(norm_mix), "w_main": w_main, "w_lat": w_lat,
        "q_norm": vec(q_norm), "w_uq": uq.reshape(depth, Q_LORA, MLA_HEADS * LANES).astype(BF16),
        "kv_norm": vec(kv_norm), "w_uk": uk.reshape(depth, KV_LORA, MLA_HEADS * LANES).astype(BF16),
        "w_uv": uv.reshape(depth, KV_LORA, D_ATTN).astype(BF16),
        "ssm_conv_w": ssm_conv_w, "ssm_conv_b": vec(ssm_conv_b),
        "dt_bias": lane_pad(dt_bias), "a_log": lane_pad(a_log),
        "d_skip": jnp.repeat(d_skip, SSM_HEAD_DIM, axis=-1).reshape(depth, 1, D_SSM),
        "ssm_norm": vec(ssm_norm), "attn_out_norm": vec(attn_out_norm),
        "w_out": w_out.astype(BF16), "norm_mem_q": vec(norm_mem_q), "w_mq": w_mq.astype(BF16),
        "w_mo": w_mo.astype(BF16), "norm_ffn": vec(norm_ffn), "w_up": w_up.astype(BF16),
        "ffn_conv_w": ffn_conv_w, "ffn_conv_b": vec(ffn_conv_b), "w_down": w_down.astype(BF16),
    }


def kernel(x, mem, positions, norm_mix, w_in, ssm_conv_w, ssm_conv_b, dt_bias, a_log, d_skip,
           ssm_norm, q_norm, w_uq, kv_norm, w_ukv, attn_out_norm, w_out, norm_mem_q, norm_mem_kv,
           w_mq, w_mk, w_mv, w_mo, norm_ffn, w_up, ffn_conv_w, ffn_conv_b, w_down, final_norm):
    batch, seq, _ = x.shape
    depth = w_in.shape[0]
    p = _prep_params(norm_mix, w_in, ssm_conv_w, ssm_conv_b, dt_bias, a_log, d_skip, ssm_norm,
                     q_norm, w_uq, kv_norm, w_ukv, attn_out_norm, w_out, norm_mem_q, w_mq, w_mo,
                     norm_ffn, w_up, ffn_conv_w, ffn_conv_b, w_down)
    ctab, stab = _rope_tables(positions)
    kmem, vmem = _mem_kv(mem, norm_mem_kv.reshape(depth, 1, D_MODEL), w_mk.astype(BF16),
                         w_mv.astype(BF16))
    fnorm = final_norm.reshape(1, D_MODEL)
    tm_in = min(512, seq)
    tm = min(512, seq)
    tc = min(512, seq)
    x2d = x.reshape(batch * seq, D_MODEL)
    for l in range(depth):
        z, xbc, sm, q, k, v = _mix_in(x2d, l, p, ctab, stab, seq, tm_in)
        ys = _ssd(xbc, z, sm, l, p, batch, seq, tc)
        ya = _mla_attn(q, k, v, batch, seq)
        x2d = _mix_out_mem(x2d, ys, ya, kmem, vmem, l, p, batch, seq, tm)
        x2d = _ffn(x2d, l, p, fnorm, batch, seq, tm, final=(l == depth - 1))
    return x2d.reshape(batch, seq, D_MODEL)
```

```python
import functools
import math

import jax
import jax.numpy as jnp
from jax import lax
from jax.experimental import pallas as pl
from jax.experimental.pallas import tpu as pltpu

F32 = jnp.float32
BF16 = jnp.bfloat16

D_MODEL = 1024
EPS = 1e-6
SSM_HEADS = 16
SSM_HEAD_DIM = 64
D_SSM = SSM_HEADS * SSM_HEAD_DIM
SSM_GROUPS = 4
SSM_STATE = 128
SSM_CONV = 4
SSM_CHUNK = 128
CONV_CH = D_SSM + 2 * SSM_GROUPS * SSM_STATE
MLA_HEADS = 16
QK_NOPE = 64
QK_ROPE = 32
V_DIM = 64
Q_LORA = 384
KV_LORA = 256
D_ATTN = MLA_HEADS * V_DIM
ROPE_THETA = 10000.0
MEM_HEADS = 4
MEM_HEAD_DIM = D_MODEL // MEM_HEADS
D_FF = 2816
FFN_CONV = 3

LANES = 128
SUBLANES = 8
MXU_TILE = 256
VMEM_LIMIT = 56 * 1024 * 1024

ROPE_LO = QK_NOPE
ROPE_MID = QK_NOPE + QK_ROPE // 2
ROPE_HI = QK_NOPE + QK_ROPE
DT_LO = ROPE_HI
DT_HI = ROPE_HI + SSM_HEADS

FFN_CHUNK = 256
CONV_CHUNK = 256
assert SSM_CONV == 4 and CONV_CH % CONV_CHUNK == 0
ATTN_SUB = 256
ATTN_STEP_SUBS = 2
ATTN_HEADS = 8
PV_WIDTH = 256


def _dot(a, b):
    return jnp.dot(a, b, preferred_element_type=F32)


def _dot_nt(a, b):
    return lax.dot_general(a, b, (((1,), (1,)), ((), ())), preferred_element_type=F32)


def _rms(xf, g):
    var = jnp.mean(xf * xf, axis=-1, keepdims=True)
    return xf * lax.rsqrt(var + EPS) * g


def _silu(x):
    return x * (1.0 / (1.0 + jnp.exp(-x)))


def _softplus(x):
    return jnp.maximum(x, 0.0) + jnp.log1p(jnp.exp(-jnp.abs(x)))


def _rope128(t, c, s, lane):
    rot = jnp.where(lane < ROPE_MID, pltpu.roll(t, LANES - QK_ROPE // 2, 1),
                    pltpu.roll(t, QK_ROPE // 2, 1))
    return t * c + rot * s


def _params(sem):
    return pltpu.CompilerParams(dimension_semantics=sem, vmem_limit_bytes=VMEM_LIMIT)


def _const_spec(shape, index):
    return pl.BlockSpec(shape, lambda *_: index, pipeline_mode=pl.Buffered(1))


def _rope_kernel(pos_ref, inv_ref, cos_ref, sin_ref):
    ang = pos_ref[...] * inv_ref[...]
    cos_ref[...] = jnp.cos(ang)
    sin_ref[...] = jnp.sin(ang)


def _rope_tables(positions):
    t = positions.size
    half = QK_ROPE // 2
    per_row = LANES // half
    inv_freq = 1.0 / (ROPE_THETA ** (jnp.arange(0, QK_ROPE, 2, dtype=F32) / QK_ROPE))
    posd = jnp.repeat(positions.reshape(t).astype(F32), half).reshape(t // per_row, LANES)
    invd = jnp.tile(inv_freq, per_row).reshape(1, LANES)
    rows = t // per_row
    cosd, sind = pl.pallas_call(
        _rope_kernel,
        out_shape=(jax.ShapeDtypeStruct((rows, LANES), F32),) * 2,
        name="rope_tables",
    )(posd, invd)
    cos = cosd.reshape(t, half)
    sin = sind.reshape(t, half)
    ctab = jnp.concatenate([jnp.ones((t, QK_NOPE), F32), cos, cos,
                            jnp.zeros((t, LANES - ROPE_HI), F32)], axis=-1)
    stab = jnp.concatenate([jnp.zeros((t, QK_NOPE), F32), -sin, sin,
                            jnp.zeros((t, LANES - ROPE_HI), F32)], axis=-1)
    return ctab, stab


_O_Z = 0
_O_XBC = D_SSM
_W_MAIN = _O_XBC + CONV_CH
_L_CQ = 0
_L_CKV = Q_LORA
_L_SM = Q_LORA + KV_LORA
_W_LAT = _L_SM + LANES
_QK_SCALE = (QK_NOPE + QK_ROPE) ** -0.5 * math.log2(math.e)


def _mix_in_kernel(x_ref, xh_ref, g_ref, wm_ref, wl_ref, cw_ref, cb_ref, qg_ref, wuq_ref, kg_ref,
                   wuk_ref, wuv_ref, c_ref, s_ref, z_ref, xbc_ref, sm_ref, q_ref, k_ref, v_ref,
                   *, tiles_per_seq):
    first = pl.program_id(0) % tiles_per_seq == 0
    xh = jnp.where(first, 0.0, xh_ref[...])
    h_full = _rms(jnp.concatenate([xh, x_ref[...]], axis=0), g_ref[...]).astype(BF16)
    h = h_full[SUBLANES:, :]

    def xbc_dot(cc):
        lo = _O_XBC + cc * CONV_CHUNK
        return _dot(h_full, wm_ref[:, lo:lo + CONV_CHUNK])

    def conv_store(cc, u):
        cols = slice(cc * CONV_CHUNK, (cc + 1) * CONV_CHUNK)
        cw = cw_ref[:, cols]
        u1 = pltpu.roll(u, 1, 0)
        near = cw[3:4, :] * u + cw[2:3, :] * u1
        far = cw[1:2, :] * u + cw[0:1, :] * u1
        acc = cb_ref[:, cols] + near[SUBLANES:, :] + pltpu.roll(far, 2, 0)[SUBLANES:, :]
        xbc_ref[:, cols] = _silu(acc).astype(BF16)

    w = MXU_TILE
    heads_per_tile = w // LANES

    u = xbc_dot(0)
    cq = _dot(h, wl_ref[:, _L_CQ:_L_CKV])
    ckv = _dot(h, wl_ref[:, _L_CKV:_L_SM])
    sm = _dot(h, wl_ref[:, _L_SM:_W_LAT])
    sm_ref[...] = sm
    c = c_ref[...]
    s = s_ref[...]
    lane = lax.broadcasted_iota(jnp.int32, c.shape, 1)
    cs = c * _QK_SCALE
    ss = s * _QK_SCALE
    cqn = _rms(cq, qg_ref[...]).astype(BF16)
    ckvn = _rms(ckv, kg_ref[...]).astype(BF16)
    kpe = _rope128(sm, c, s, lane)
    kpe = jnp.where((lane >= ROPE_LO) & (lane < ROPE_HI), kpe, 0.0)

    def z_tile(j):
        z_ref[:, j * w:(j + 1) * w] = _dot(h, wm_ref[:, _O_Z + j * w:_O_Z + (j + 1) * w]).astype(BF16)

    def q_tile(j):
        qq = _dot(cqn, wuq_ref[:, j * w:(j + 1) * w])
        for r in range(heads_per_tile):
            sl = slice(j * w + r * LANES, j * w + (r + 1) * LANES)
            q_ref[:, sl] = _rope128(qq[:, r * LANES:(r + 1) * LANES], cs, ss, lane).astype(BF16)

    def k_tile(j):
        kk = _dot(ckvn, wuk_ref[:, j * w:(j + 1) * w])
        for r in range(heads_per_tile):
            sl = slice(j * w + r * LANES, j * w + (r + 1) * LANES)
            k_ref[:, sl] = (kk[:, r * LANES:(r + 1) * LANES] + kpe).astype(BF16)

    def v_tile(j):
        v_ref[:, j * w:(j + 1) * w] = _dot(ckvn, wuv_ref[:, j * w:(j + 1) * w]).astype(BF16)

    nqk = MLA_HEADS // heads_per_tile
    tasks = ([functools.partial(z_tile, j) for j in range(D_SSM // w)]
             + [functools.partial(q_tile, j) for j in range(nqk)]
             + [functools.partial(k_tile, j) for j in range(nqk)]
             + [functools.partial(v_tile, j) for j in range(D_ATTN // w)])
    nconv = CONV_CH // CONV_CHUNK
    per = -(-len(tasks) // nconv)
    for cc in range(nconv):
        nxt = xbc_dot(cc + 1) if cc + 1 < nconv else None
        conv_store(cc, u)
        for task in tasks[cc * per:(cc + 1) * per]:
            task()
        u = nxt


def _mix_in(x2d, l, p, ctab, stab, seq, tm):
    t = x2d.shape[0]
    hp = MLA_HEADS * LANES
    row = lambda w: pl.BlockSpec((tm, w), lambda i: (i, 0))
    halo = pl.BlockSpec((SUBLANES, D_MODEL),
                        lambda i: (jnp.maximum(i * (tm // SUBLANES) - 1, 0), 0))
    lay = lambda shape: _const_spec((None,) + shape, (l, 0, 0))
    return pl.pallas_call(
        functools.partial(_mix_in_kernel, tiles_per_seq=seq // tm),
        grid=(t // tm,),
        in_specs=[row(D_MODEL), halo, lay((1, D_MODEL)), lay((D_MODEL, _W_MAIN)),
                  lay((D_MODEL, _W_LAT)), lay((SSM_CONV, CONV_CH)), lay((1, CONV_CH)),
                  lay((1, Q_LORA)), lay((Q_LORA, hp)), lay((1, KV_LORA)), lay((KV_LORA, hp)),
                  lay((KV_LORA, D_ATTN)), row(LANES), row(LANES)],
        out_specs=[row(D_SSM), row(CONV_CH), row(LANES), row(hp), row(hp), row(D_ATTN)],
        out_shape=[jax.ShapeDtypeStruct((t, D_SSM), BF16), jax.ShapeDtypeStruct((t, CONV_CH), BF16),
                   jax.ShapeDtypeStruct((t, LANES), F32), jax.ShapeDtypeStruct((t, hp), BF16),
                   jax.ShapeDtypeStruct((t, hp), BF16), jax.ShapeDtypeStruct((t, D_ATTN), BF16)],
        compiler_params=_params(("parallel",)),
        name="mix_in",
    )(x2d, x2d, p["norm_mix"], p["w_main"], p["w_lat"], p["ssm_conv_w"], p["ssm_conv_b"],
      p["q_norm"], p["w_uq"], p["kv_norm"], p["w_uk"], p["w_uv"], ctab, stab)


def _split3(a):
    hi = a.astype(BF16)
    r = a - hi.astype(F32)
    mid = r.astype(BF16)
    lo = (r - mid.astype(F32)).astype(BF16)
    return hi, mid, lo


def _ssd_kernel(xbc_ref, z_ref, sm_ref, dtb_ref, alog_ref, dsk_ref, g_ref, o_ref, state_ref, *, tc):
    i = pl.program_id(1)

    @pl.when(i == 0)
    def _():
        state_ref[...] = jnp.zeros_like(state_ref)

    xc = xbc_ref[...].astype(F32)

    n = SSM_CHUNK
    ri = lax.broadcasted_iota(jnp.int32, (n, n), 0)
    ci = lax.broadcasted_iota(jnp.int32, (n, n), 1)
    causal = ri >= ci
    tri = jnp.where(causal, 1.0, 0.0).astype(BF16)
    lane = lax.broadcasted_iota(jnp.int32, (1, LANES), 1)
    dt_lanes = (lane >= DT_LO) & (lane < DT_HI)
    a_tile = jnp.where(dt_lanes, -jnp.exp(alog_ref[...]), 0.0)
    dtb = dtb_ref[...]
    bo = D_SSM
    co = D_SSM + SSM_GROUPS * SSM_STATE
    heads_per_group = SSM_HEADS // SSM_GROUPS
    lo_half = lax.broadcasted_iota(jnp.int32, (n, LANES), 1) < SSM_HEAD_DIM
    lo_row = lax.broadcasted_iota(jnp.int32, (1, LANES), 1) < SSM_HEAD_DIM

    for c in range(tc // n):
        rows = slice(c * n, (c + 1) * n)
        xs = xc[rows, 0:D_SSM]
        dtf = _softplus(sm_ref[rows, :] + dtb)
        ad = dtf * a_tile
        hi, mid, lo = _split3(ad)
        cs = _dot(tri, hi) + _dot(tri, mid) + _dot(tri, lo)
        cs_t = cs.T
        dt_t = dtf.T
        ys = []
        for g in range(SSM_GROUPS):
            bg = xc[rows, bo + g * SSM_STATE: bo + (g + 1) * SSM_STATE]
            cg = xc[rows, co + g * SSM_STATE: co + (g + 1) * SSM_STATE]
            bg_t = bg.T
            cg16 = cg.astype(BF16)
            cb = _dot_nt(cg16, bg.astype(BF16))
            for q in range(heads_per_group // 2):
                pair = g * (heads_per_group // 2) + q
                xp = xs[:, pair * LANES:(pair + 1) * LANES]
                halves = (jnp.where(lo_half, xp, 0.0).astype(BF16),
                          jnp.where(lo_half, 0.0, xp).astype(BF16))
                prev = state_ref[pair]
                y_diag = st_new = None
                e_cols, e_last = [], []
                for k in range(2):
                    ln = DT_LO + 2 * pair + k
                    col = cs[:, ln:ln + 1]
                    row = cs_t[ln:ln + 1, :]
                    dt_row = dt_t[ln:ln + 1, :]
                    last = cs[n - 1:n, ln:ln + 1]
                    lmat = jnp.exp(jnp.where(causal, col - row, -jnp.inf))
                    mm = (cb * lmat * dt_row).astype(BF16)
                    w_row = jnp.exp(last - row) * dt_row
                    yd = _dot(mm, halves[k])
                    sn = _dot((bg_t * w_row).astype(BF16), halves[k])
                    y_diag = yd if k == 0 else y_diag + yd
                    st_new = sn if k == 0 else st_new + sn
                    e_cols.append(jnp.exp(col))
                    e_last.append(jnp.exp(last))
                y_off = _dot(cg16, prev.astype(BF16))
                ys.append(y_diag + jnp.where(lo_half, e_cols[0], e_cols[1]) * y_off)
                state_ref[pair] = prev * jnp.where(lo_row, e_last[0], e_last[1]) + st_new
        y = jnp.concatenate(ys, axis=-1) + xs * dsk_ref[...]
        y = y * _silu(z_ref[rows, :].astype(F32))
        o_ref[rows, :] = _rms(y, g_ref[...]).astype(BF16)


def _ssd(xbc, z, sm, l, p, batch, seq, tc):
    ns = seq // tc
    main = lambda w: pl.BlockSpec((tc, w), lambda b, i: (b * ns + i, 0))
    lay = lambda shape: _const_spec((None,) + shape, (l, 0, 0))
    return pl.pallas_call(
        functools.partial(_ssd_kernel, tc=tc),
        grid=(batch, ns),
        in_specs=[main(CONV_CH), main(D_SSM), main(LANES), lay((1, LANES)), lay((1, LANES)),
                  lay((1, D_SSM)), lay((1, D_SSM))],
        out_specs=main(D_SSM),
        out_shape=jax.ShapeDtypeStruct((batch * seq, D_SSM), BF16),
        scratch_shapes=[pltpu.VMEM((SSM_HEADS // 2, SSM_STATE, 2 * SSM_HEAD_DIM), F32)],
        compiler_params=_params(("arbitrary", "arbitrary")),
        name="ssd",
    )(xbc, z, sm, p["dt_bias"], p["a_log"], p["d_skip"], p["ssm_norm"])


def _attn_tile(q_ref, k_ref, v_ref, o_ref, *, c):
    tq = ATTN_SUB
    ri = lax.broadcasted_iota(jnp.int32, (tq, tq), 0)
    ci = lax.broadcasted_iota(jnp.int32, (tq, tq), 1)
    diag_mask = ri >= ci
    lane = lax.broadcasted_iota(jnp.int32, (tq, PV_WIDTH), 1)
    heads_per_pv = PV_WIDTH // V_DIM
    for sub in range(ATTN_STEP_SUBS):
        t = c * ATTN_STEP_SUBS + sub
        kv_len = (t + 1) * tq
        rows = slice(sub * tq, (sub + 1) * tq)
        out = None
        for hh in range(ATTN_HEADS):
            hs = slice(hh * LANES, (hh + 1) * LANES)
            r = hh % heads_per_pv
            vcols = slice((hh - r) * V_DIM, (hh - r) * V_DIM + PV_WIDTH)
            s = _dot_nt(q_ref[rows, hs], k_ref[0:kv_len, hs])
            diag = jnp.where(diag_mask, s[:, t * tq:], -jnp.inf)
            s = diag if t == 0 else jnp.concatenate([s[:, :t * tq], diag], axis=-1)
            m = jnp.max(s, axis=-1, keepdims=True)
            pr = jnp.exp2(s - m)
            l = jnp.sum(pr, axis=-1, keepdims=True)
            o = _dot(pr.astype(BF16), v_ref[0:kv_len, vcols]) / l
            out = o if r == 0 else jnp.where(lane >= r * V_DIM, o, out)
            if r == heads_per_pv - 1:
                o_ref[rows, vcols] = out.astype(BF16)


def _attn_kernel(q_ref, k_ref, v_ref, o_ref, *, nsteps):
    i = pl.program_id(2)
    for c in range(nsteps):
        pl.when(i == c)(functools.partial(_attn_tile, q_ref, k_ref, v_ref, o_ref, c=c))


def _mla_attn(q, k, v, batch, seq):
    tstep = ATTN_SUB * ATTN_STEP_SUBS
    nsteps = seq // tstep
    qw = ATTN_HEADS * LANES
    vw = ATTN_HEADS * V_DIM
    return pl.pallas_call(
        functools.partial(_attn_kernel, nsteps=nsteps),
        grid=(batch, MLA_HEADS // ATTN_HEADS, nsteps),
        in_specs=[pl.BlockSpec((tstep, qw), lambda b, j, i: (b * nsteps + i, j)),
                  pl.BlockSpec((seq, qw), lambda b, j, i: (b, j)),
                  pl.BlockSpec((seq, vw), lambda b, j, i: (b, j))],
        out_specs=pl.BlockSpec((tstep, vw), lambda b, j, i: (b * nsteps + i, j)),
        out_shape=jax.ShapeDtypeStruct((batch * seq, D_ATTN), BF16),
        compiler_params=_params(("parallel", "parallel", "arbitrary")),
        name="mla_attn",
    )(q, k, v)


def _mem_kv_kernel(m_ref, g_ref, wk_ref, wv_ref, k_ref, v_ref):
    h = _rms(m_ref[...], g_ref[...]).astype(BF16)
    k_ref[...] = _dot(h, wk_ref[...]).astype(BF16)
    v_ref[...] = _dot(h, wv_ref[...]).astype(BF16)


def _mem_kv(mem, g, wk, wv):
    depth = g.shape[0]
    batch, ml, _ = mem.shape
    rows = batch * ml
    lay = lambda shape: pl.BlockSpec((None,) + shape, lambda l: (l, 0, 0))
    return pl.pallas_call(
        _mem_kv_kernel,
        grid=(depth,),
        in_specs=[pl.BlockSpec((rows, D_MODEL), lambda l: (0, 0)),
                  lay((1, D_MODEL)), lay((D_MODEL, D_MODEL)), lay((D_MODEL, D_MODEL))],
        out_specs=[lay((rows, D_MODEL)), lay((rows, D_MODEL))],
        out_shape=[jax.ShapeDtypeStruct((depth, rows, D_MODEL), BF16)] * 2,
        compiler_params=_params(("arbitrary",)),
        name="mem_kv",
    )(mem.reshape(rows, D_MODEL), g, wk, wv)


def _mix_out_mem_kernel(x_ref, ys_ref, ya_ref, ag_ref, wo_ref, mg_ref, wq_ref, km_ref, vm_ref,
                        wmo_ref, o_ref):
    ya = _rms(ya_ref[...].astype(F32), ag_ref[...]).astype(BF16)
    x1 = x_ref[...] + _dot(ys_ref[...], wo_ref[0:D_SSM, :]) + _dot(ya, wo_ref[D_SSM:, :])
    hq = _rms(x1, mg_ref[...]).astype(BF16)
    qm = (_dot(hq, wq_ref[...]) * (MEM_HEAD_DIM ** -0.5)).astype(BF16)
    outs = []
    for hd in range(MEM_HEADS):
        hs = slice(hd * MEM_HEAD_DIM, (hd + 1) * MEM_HEAD_DIM)
        s = _dot_nt(qm[:, hs], km_ref[:, hs])
        m = jnp.max(s, axis=-1, keepdims=True)
        pr = jnp.exp(s - m)
        l = jnp.sum(pr, axis=-1, keepdims=True)
        outs.append(_dot(pr.astype(BF16), vm_ref[:, hs]) / l)
    o = jnp.concatenate(outs, axis=-1).astype(BF16)
    o_ref[...] = x1 + _dot(o, wmo_ref[...])


def _mix_out_mem(x2d, ys, ya, kmem, vmem, l, p, batch, seq, tm):
    ns = seq // tm
    ml = kmem.shape[1] // batch
    row = lambda w: pl.BlockSpec((tm, w), lambda b, i: (b * ns + i, 0))
    lay = lambda shape: _const_spec((None,) + shape, (l, 0, 0))
    memspec = pl.BlockSpec((None, ml, D_MODEL), lambda b, i: (l, b, 0))
    return pl.pallas_call(
        _mix_out_mem_kernel,
        grid=(batch, ns),
        in_specs=[row(D_MODEL), row(D_SSM), row(D_ATTN), lay((1, D_ATTN)),
                  lay((D_SSM + D_ATTN, D_MODEL)), lay((1, D_MODEL)), lay((D_MODEL, D_MODEL)),
                  memspec, memspec, lay((D_MODEL, D_MODEL))],
        out_specs=row(D_MODEL),
        out_shape=jax.ShapeDtypeStruct(x2d.shape, F32),
        compiler_params=_params(("parallel", "parallel")),
        name="mix_out_mem",
    )(x2d, ys, ya, p["attn_out_norm"], p["w_out"], p["norm_mem_q"], p["w_mq"], kmem, vmem,
      p["w_mo"])


def _ffn_kernel(x_ref, xh_ref, g_ref, wup_ref, cw_ref, cb_ref, wdn_ref, fg_ref, o_ref, act_ref, *,
                final):
    i = pl.program_id(1)
    x = x_ref[...]
    xh = jnp.where(i > 0, xh_ref[...], 0.0)
    h = _rms(jnp.concatenate([xh, x], axis=0), g_ref[...]).astype(BF16)

    def conv(u, cols):
        w = cw_ref[:, cols]
        out = cb_ref[:, cols] + w[FFN_CONV - 1:FFN_CONV, :] * u[SUBLANES:, :]
        for j in range(FFN_CONV - 1):
            shift = FFN_CONV - 1 - j
            out = out + w[j:j + 1, :] * pltpu.roll(u, shift, 0)[SUBLANES:, :]
        return out

    nchunks = D_FF // FFN_CHUNK
    gcols = lambda c: slice(c * FFN_CHUNK, (c + 1) * FFN_CHUNK)
    vcols = lambda c: slice(D_FF + c * FFN_CHUNK, D_FF + (c + 1) * FFN_CHUNK)
    up = lambda c: (_dot(h, wup_ref[:, gcols(c)]), _dot(h, wup_ref[:, vcols(c)]))

    nxt = up(0)
    for c in range(nchunks):
        ug, uv = nxt
        if c + 1 < nchunks:
            nxt = up(c + 1)
        act_ref[:, gcols(c)] = (_silu(conv(ug, gcols(c))) * conv(uv, vcols(c))).astype(BF16)
    acc = x + _dot(act_ref[...], wdn_ref[...])
    if final:
        acc = _rms(acc, fg_ref[...])
    o_ref[...] = acc


def _ffn(x2d, l, p, final_norm, batch, seq, tm, final):
    ns = seq // tm
    halo_blocks_per_seq = seq // SUBLANES
    row = pl.BlockSpec((tm, D_MODEL), lambda b, i: (b * ns + i, 0))
    halo = pl.BlockSpec(
        (SUBLANES, D_MODEL),
        lambda b, i: (jnp.maximum(b * halo_blocks_per_seq + i * (tm // SUBLANES) - 1, 0), 0))
    lay = lambda shape: _const_spec((None,) + shape, (l, 0, 0))
    return pl.pallas_call(
        functools.partial(_ffn_kernel, final=final),
        grid=(batch, ns),
        in_specs=[row, halo, lay((1, D_MODEL)), lay((D_MODEL, 2 * D_FF)), lay((FFN_CONV, 2 * D_FF)),
                  lay((1, 2 * D_FF)), lay((D_FF, D_MODEL)), _const_spec((1, D_MODEL), (0, 0))],
        out_specs=row,
        out_shape=jax.ShapeDtypeStruct(x2d.shape, F32),
        compiler_params=_params(("parallel", "parallel")),
        scratch_shapes=[pltpu.VMEM((tm, D_FF), BF16)],
        name="ffn_final" if final else "ffn",
    )(x2d, x2d, p["norm_ffn"], p["w_up"], p["ffn_conv_w"], p["ffn_conv_b"], p["w_down"], final_norm)


def _prep_params(norm_mix, w_in, ssm_conv_w, ssm_conv_b, dt_bias, a_log, d_skip, ssm_norm, q_norm,
                 w_uq, kv_norm, w_ukv, attn_out_norm, w_out, norm_mem_q, w_mq, w_mo, norm_ffn, w_up,
                 ffn_conv_w, ffn_conv_b, w_down):
    depth = w_in.shape[0]
    o1 = D_SSM
    o2 = o1 + CONV_CH
    o3 = o2 + SSM_HEADS
    o4 = o3 + Q_LORA
    o5 = o4 + KV_LORA
    vec = lambda a: a.reshape(depth, 1, -1)
    w_main = w_in[..., :o2].astype(BF16)
    zeros = lambda w: jnp.zeros((depth, D_MODEL, w), w_in.dtype)
    w_lat = jnp.concatenate([w_in[..., o3:o5], zeros(ROPE_LO), w_in[..., o5:], w_in[..., o2:o3],
                             zeros(LANES - DT_HI)], axis=-1).astype(BF16)
    uq = w_uq.reshape(depth, Q_LORA, MLA_HEADS, QK_NOPE + QK_ROPE)
    uq = jnp.pad(uq, ((0, 0), (0, 0), (0, 0), (0, LANES - QK_NOPE - QK_ROPE)))
    ukv = w_ukv.reshape(depth, KV_LORA, MLA_HEADS, QK_NOPE + V_DIM)
    uk = jnp.pad(ukv[..., :QK_NOPE], ((0, 0), (0, 0), (0, 0), (0, LANES - QK_NOPE)))
    uv = ukv[..., QK_NOPE:]
    lane_pad = lambda a: jnp.pad(a, ((0, 0), (DT_LO, LANES - DT_HI))).reshape(depth, 1, LANES)
    return {
        "norm_mix": vec(norm_mix), "w_main": w_main, "w_lat": w_lat,
        "q_norm": vec(q_norm), "w_uq": uq.reshape(depth, Q_LORA, MLA_HEADS * LANES).astype(BF16),
        "kv_norm": vec(kv_norm), "w_uk": uk.reshape(depth, KV_LORA, MLA_HEADS * LANES).astype(BF16),
        "w_uv": uv.reshape(depth, KV_LORA, D_ATTN).astype(BF16),
        "ssm_conv_w": ssm_conv_w, "ssm_conv_b": vec(ssm_conv_b),
        "dt_bias": lane_pad(dt_bias), "a_log": lane_pad(a_log),
        "d_skip": jnp.repeat(d_skip, SSM_HEAD_DIM, axis=-1).reshape(depth, 1, D_SSM),
        "ssm_norm": vec(ssm_norm), "attn_out_norm": vec(attn_out_norm),
        "w_out": w_out.astype(BF16), "norm_mem_q": vec(norm_mem_q), "w_mq": w_mq.astype(BF16),
        "w_mo": w_mo.astype(BF16), "norm_ffn": vec(norm_ffn), "w_up": w_up.astype(BF16),
        "ffn_conv_w": ffn_conv_w, "ffn_conv_b": vec(ffn_conv_b), "w_down": w_down.astype(BF16),
    }


def kernel(x, mem, positions, norm_mix, w_in, ssm_conv_w, ssm_conv_b, dt_bias, a_log, d_skip,
           ssm_norm, q_norm, w_uq, kv_norm, w_ukv, attn_out_norm, w_out, norm_mem_q, norm_mem_kv,
           w_mq, w_mk, w_mv, w_mo, norm_ffn, w_up, ffn_conv_w, ffn_conv_b, w_down, final_norm):
    batch, seq, _ = x.shape
    depth = w_in.shape[0]
    p = _prep_params(norm_mix, w_in, ssm_conv_w, ssm_conv_b, dt_bias, a_log, d_skip, ssm_norm,
                     q_norm, w_uq, kv_norm, w_ukv, attn_out_norm, w_out, norm_mem_q, w_mq, w_mo,
                     norm_ffn, w_up, ffn_conv_w, ffn_conv_b, w_down)
    ctab, stab = _rope_tables(positions)
    kmem, vmem = _mem_kv(mem, norm_mem_kv.reshape(depth, 1, D_MODEL), w_mk.astype(BF16),
                         w_mv.astype(BF16))
    fnorm = final_norm.reshape(1, D_MODEL)
    tm_in = min(512, seq)
    tm = min(1024, seq)
    tc = min(512, seq)
    x2d = x.reshape(batch * seq, D_MODEL)
    for l in range(depth):
        z, xbc, sm, q, k, v = _mix_in(x2d, l, p, ctab, stab, seq, tm_in)
        ys = _ssd(xbc, z, sm, l, p, batch, seq, tc)
        ya = _mla_attn(q, k, v, batch, seq)
        x2d = _mix_out_mem(x2d, ys, ya, kmem, vmem, l, p, batch, seq, tm)
        x2d = _ffn(x2d, l, p, fnorm, batch, seq, tm, final=(l == depth - 1))
    return x2d.reshape(batch, seq, D_MODEL)
```

```python
import functools
import math

import jax
import jax.numpy as jnp
from jax import lax
from jax.experimental import pallas as pl
from jax.experimental.pallas import tpu as pltpu

F32 = jnp.float32
BF16 = jnp.bfloat16

D_MODEL = 1024
EPS = 1e-6
SSM_HEADS = 16
SSM_HEAD_DIM = 64
D_SSM = SSM_HEADS * SSM_HEAD_DIM
SSM_GROUPS = 4
SSM_STATE = 128
SSM_CONV = 4
SSM_CHUNK = 128
CONV_CH = D_SSM + 2 * SSM_GROUPS * SSM_STATE
MLA_HEADS = 16
QK_NOPE = 64
QK_ROPE = 32
V_DIM = 64
Q_LORA = 384
KV_LORA = 256
D_ATTN = MLA_HEADS * V_DIM
ROPE_THETA = 10000.0
MEM_HEADS = 4
MEM_HEAD_DIM = D_MODEL // MEM_HEADS
D_FF = 2816
FFN_CONV = 3

LANES = 128
SUBLANES = 8
MXU_TILE = 256
VMEM_LIMIT = 56 * 1024 * 1024

ROPE_LO = QK_NOPE
ROPE_MID = QK_NOPE + QK_ROPE // 2
ROPE_HI = QK_NOPE + QK_ROPE
DT_LO = ROPE_HI
DT_HI = ROPE_HI + SSM_HEADS

FFN_CHUNK = 256
CONV_CHUNK = 256
assert SSM_CONV == 4 and CONV_CH % CONV_CHUNK == 0
ATTN_SUB = 256
ATTN_STEP_SUBS = 2
ATTN_HEADS = 8
PV_WIDTH = 256


def _dot(a, b):
    return jnp.dot(a, b, preferred_element_type=F32)


def _dot_nt(a, b):
    return lax.dot_general(a, b, (((1,), (1,)), ((), ())), preferred_element_type=F32)


def _rms(xf, g):
    var = jnp.mean(xf * xf, axis=-1, keepdims=True)
    return xf * lax.rsqrt(var + EPS) * g


def _silu(x):
    h = 0.5 * x
    return h + h * jnp.tanh(h)


def _softplus(x):
    return jnp.maximum(x, 0.0) + jnp.log1p(jnp.exp(-jnp.abs(x)))


def _rope128(t, c, s, lane):
    rot = jnp.where(lane < ROPE_MID, pltpu.roll(t, LANES - QK_ROPE // 2, 1),
                    pltpu.roll(t, QK_ROPE // 2, 1))
    return t * c + rot * s


def _params(sem):
    return pltpu.CompilerParams(dimension_semantics=sem, vmem_limit_bytes=VMEM_LIMIT)


def _const_spec(shape, index):
    return pl.BlockSpec(shape, lambda *_: index, pipeline_mode=pl.Buffered(1))


def _rope_kernel(pos_ref, inv_ref, cos_ref, sin_ref):
    ang = pos_ref[...] * inv_ref[...]
    cos_ref[...] = jnp.cos(ang)
    sin_ref[...] = jnp.sin(ang)


def _rope_tables(positions):
    t = positions.size
    half = QK_ROPE // 2
    per_row = LANES // half
    inv_freq = 1.0 / (ROPE_THETA ** (jnp.arange(0, QK_ROPE, 2, dtype=F32) / QK_ROPE))
    posd = jnp.repeat(positions.reshape(t).astype(F32), half).reshape(t // per_row, LANES)
    invd = jnp.tile(inv_freq, per_row).reshape(1, LANES)
    rows = t // per_row
    cosd, sind = pl.pallas_call(
        _rope_kernel,
        out_shape=(jax.ShapeDtypeStruct((rows, LANES), F32),) * 2,
        name="rope_tables",
    )(posd, invd)
    cos = cosd.reshape(t, half)
    sin = sind.reshape(t, half)
    ctab = jnp.concatenate([jnp.ones((t, QK_NOPE), F32), cos, cos,
                            jnp.zeros((t, LANES - ROPE_HI), F32)], axis=-1)
    stab = jnp.concatenate([jnp.zeros((t, QK_NOPE), F32), -sin, sin,
                            jnp.zeros((t, LANES - ROPE_HI), F32)], axis=-1)
    return ctab, stab


_O_Z = 0
_O_XBC = D_SSM
_W_MAIN = _O_XBC + CONV_CH
_L_CQ = 0
_L_CKV = Q_LORA
_L_SM = Q_LORA + KV_LORA
_W_LAT = _L_SM + LANES
_QK_SCALE = (QK_NOPE + QK_ROPE) ** -0.5 * math.log2(math.e)


def _mix_in_kernel(x_ref, xh_ref, g_ref, wm_ref, wl_ref, cw_ref, cb_ref, qg_ref, wuq_ref, kg_ref,
                   wuk_ref, wuv_ref, c_ref, s_ref, z_ref, xbc_ref, sm_ref, q_ref, k_ref, v_ref,
                   *, tiles_per_seq):
    first = pl.program_id(0) % tiles_per_seq == 0
    xh = jnp.where(first, 0.0, xh_ref[...])
    h_full = _rms(jnp.concatenate([xh, x_ref[...]], axis=0), g_ref[...]).astype(BF16)
    h = h_full[SUBLANES:, :]

    def xbc_dot(cc):
        lo = _O_XBC + cc * CONV_CHUNK
        return _dot(h_full, wm_ref[:, lo:lo + CONV_CHUNK])

    def conv_store(cc, u):
        cols = slice(cc * CONV_CHUNK, (cc + 1) * CONV_CHUNK)
        cw = cw_ref[:, cols]
        u1 = pltpu.roll(u, 1, 0)
        near = cw[3:4, :] * u + cw[2:3, :] * u1
        far = cw[1:2, :] * u + cw[0:1, :] * u1
        acc = cb_ref[:, cols] + near[SUBLANES:, :] + pltpu.roll(far, 2, 0)[SUBLANES:, :]
        xbc_ref[:, cols] = _silu(acc).astype(BF16)

    w = MXU_TILE
    heads_per_tile = w // LANES

    u = xbc_dot(0)
    cq = _dot(h, wl_ref[:, _L_CQ:_L_CKV])
    ckv = _dot(h, wl_ref[:, _L_CKV:_L_SM])
    sm = _dot(h, wl_ref[:, _L_SM:_W_LAT])
    sm_ref[...] = sm
    c = c_ref[...]
    s = s_ref[...]
    lane = lax.broadcasted_iota(jnp.int32, c.shape, 1)
    cs = c * _QK_SCALE
    ss = s * _QK_SCALE
    cqn = _rms(cq, qg_ref[...]).astype(BF16)
    ckvn = _rms(ckv, kg_ref[...]).astype(BF16)
    kpe = _rope128(sm, c, s, lane)
    kpe = jnp.where((lane >= ROPE_LO) & (lane < ROPE_HI), kpe, 0.0)

    def z_tile(j):
        z_ref[:, j * w:(j + 1) * w] = _dot(h, wm_ref[:, _O_Z + j * w:_O_Z + (j + 1) * w]).astype(BF16)

    def q_tile(j):
        qq = _dot(cqn, wuq_ref[:, j * w:(j + 1) * w])
        for r in range(heads_per_tile):
            sl = slice(j * w + r * LANES, j * w + (r + 1) * LANES)
            q_ref[:, sl] = _rope128(qq[:, r * LANES:(r + 1) * LANES], cs, ss, lane).astype(BF16)

    def k_tile(j):
        kk = _dot(ckvn, wuk_ref[:, j * w:(j + 1) * w])
        for r in range(heads_per_tile):
            sl = slice(j * w + r * LANES, j * w + (r + 1) * LANES)
            k_ref[:, sl] = (kk[:, r * LANES:(r + 1) * LANES] + kpe).astype(BF16)

    def v_tile(j):
        v_ref[:, j * w:(j + 1) * w] = _dot(ckvn, wuv_ref[:, j * w:(j + 1) * w]).astype(BF16)

    nqk = MLA_HEADS // heads_per_tile
    tasks = ([functools.partial(z_tile, j) for j in range(D_SSM // w)]
             + [functools.partial(q_tile, j) for j in range(nqk)]
             + [functools.partial(k_tile, j) for j in range(nqk)]
             + [functools.partial(v_tile, j) for j in range(D_ATTN // w)])
    nconv = CONV_CH // CONV_CHUNK
    per = -(-len(tasks) // nconv)
    for cc in range(nconv):
        nxt = xbc_dot(cc + 1) if cc + 1 < nconv else None
        conv_store(cc, u)
        for task in tasks[cc * per:(cc + 1) * per]:
            task()
        u = nxt


def _mix_in(x2d, l, p, ctab, stab, seq, tm):
    t = x2d.shape[0]
    hp = MLA_HEADS * LANES
    row = lambda w: pl.BlockSpec((tm, w), lambda i: (i, 0))
    halo = pl.BlockSpec((SUBLANES, D_MODEL),
                        lambda i: (jnp.maximum(i * (tm // SUBLANES) - 1, 0), 0))
    lay = lambda shape: _const_spec((None,) + shape, (l, 0, 0))
    return pl.pallas_call(
        functools.partial(_mix_in_kernel, tiles_per_seq=seq // tm),
        grid=(t // tm,),
        in_specs=[row(D_MODEL), halo, lay((1, D_MODEL)), lay((D_MODEL, _W_MAIN)),
                  lay((D_MODEL, _W_LAT)), lay((SSM_CONV, CONV_CH)), lay((1, CONV_CH)),
                  lay((1, Q_LORA)), lay((Q_LORA, hp)), lay((1, KV_LORA)), lay((KV_LORA, hp)),
                  lay((KV_LORA, D_ATTN)), row(LANES), row(LANES)],
        out_specs=[row(D_SSM), row(CONV_CH), row(LANES), row(hp), row(hp), row(D_ATTN)],
        out_shape=[jax.ShapeDtypeStruct((t, D_SSM), BF16), jax.ShapeDtypeStruct((t, CONV_CH), BF16),
                   jax.ShapeDtypeStruct((t, LANES), F32), jax.ShapeDtypeStruct((t, hp), BF16),
                   jax.ShapeDtypeStruct((t, hp), BF16), jax.ShapeDtypeStruct((t, D_ATTN), BF16)],
        compiler_params=_params(("parallel",)),
        name="mix_in",
    )(x2d, x2d, p["norm_mix"], p["w_main"], p["w_lat"], p["ssm_conv_w"], p["ssm_conv_b"],
      p["q_norm"], p["w_uq"], p["kv_norm"], p["w_uk"], p["w_uv"], ctab, stab)


def _split3(a):
    hi = a.astype(BF16)
    r = a - hi.astype(F32)
    mid = r.astype(BF16)
    lo = (r - mid.astype(F32)).astype(BF16)
    return hi, mid, lo


def _ssd_kernel(xbc_ref, z_ref, sm_ref, dtb_ref, alog_ref, dsk_ref, g_ref, o_ref, state_ref, *, tc):
    i = pl.program_id(1)

    @pl.when(i == 0)
    def _():
        state_ref[...] = jnp.zeros_like(state_ref)

    xc = xbc_ref[...].astype(F32)

    n = SSM_CHUNK
    ri = lax.broadcasted_iota(jnp.int32, (n, n), 0)
    ci = lax.broadcasted_iota(jnp.int32, (n, n), 1)
    causal = ri >= ci
    tri = jnp.where(causal, 1.0, 0.0).astype(BF16)
    lane = lax.broadcasted_iota(jnp.int32, (1, LANES), 1)
    dt_lanes = (lane >= DT_LO) & (lane < DT_HI)
    a_tile = jnp.where(dt_lanes, -jnp.exp(alog_ref[...]), 0.0)
    dtb = dtb_ref[...]
    bo = D_SSM
    co = D_SSM + SSM_GROUPS * SSM_STATE
    heads_per_group = SSM_HEADS // SSM_GROUPS
    lo_half = lax.broadcasted_iota(jnp.int32, (n, LANES), 1) < SSM_HEAD_DIM
    lo_row = lax.broadcasted_iota(jnp.int32, (1, LANES), 1) < SSM_HEAD_DIM

    for c in range(tc // n):
        rows = slice(c * n, (c + 1) * n)
        xs = xc[rows, 0:D_SSM]
        dtf = _softplus(sm_ref[rows, :] + dtb)
        ad = dtf * a_tile
        hi, mid, lo = _split3(ad)
        cs = _dot(tri, hi) + _dot(tri, mid) + _dot(tri, lo)
        cs_t = cs.T
        dt_t = dtf.T
        ys = []
        for g in range(SSM_GROUPS):
            bg = xc[rows, bo + g * SSM_STATE: bo + (g + 1) * SSM_STATE]
            cg = xc[rows, co + g * SSM_STATE: co + (g + 1) * SSM_STATE]
            bg_t = bg.T
            cg16 = cg.astype(BF16)
            cb = _dot_nt(cg16, bg.astype(BF16))
            for q in range(heads_per_group // 2):
                pair = g * (heads_per_group // 2) + q
                xp = xs[:, pair * LANES:(pair + 1) * LANES]
                halves = (jnp.where(lo_half, xp, 0.0).astype(BF16),
                          jnp.where(lo_half, 0.0, xp).astype(BF16))
                prev = state_ref[pair]
                y_diag = st_new = None
                e_cols, e_last = [], []
                for k in range(2):
                    ln = DT_LO + 2 * pair + k
                    col = cs[:, ln:ln + 1]
                    row = cs_t[ln:ln + 1, :]
                    dt_row = dt_t[ln:ln + 1, :]
                    last = cs[n - 1:n, ln:ln + 1]
                    lmat = jnp.exp(jnp.where(causal, col - row, -jnp.inf))
                    mm = (cb * lmat * dt_row).astype(BF16)
                    w_row = jnp.exp(last - row) * dt_row
                    yd = _dot(mm, halves[k])
                    sn = _dot((bg_t * w_row).astype(BF16), halves[k])
                    y_diag = yd if k == 0 else y_diag + yd
                    st_new = sn if k == 0 else st_new + sn
                    e_cols.append(jnp.exp(col))
                    e_last.append(jnp.exp(last))
                y_off = _dot(cg16, prev.astype(BF16))
                ys.append(y_diag + jnp.where(lo_half, e_cols[0], e_cols[1]) * y_off)
                state_ref[pair] = prev * jnp.where(lo_row, e_last[0], e_last[1]) + st_new
        y = jnp.concatenate(ys, axis=-1) + xs * dsk_ref[...]
        y = y * _silu(z_ref[rows, :].astype(F32))
        o_ref[rows, :] = _rms(y, g_ref[...]).astype(BF16)


def _ssd(xbc, z, sm, l, p, batch, seq, tc):
    ns = seq // tc
    main = lambda w: pl.BlockSpec((tc, w), lambda b, i: (b * ns + i, 0))
    lay = lambda shape: _const_spec((None,) + shape, (l, 0, 0))
    return pl.pallas_call(
        functools.partial(_ssd_kernel, tc=tc),
        grid=(batch, ns),
        in_specs=[main(CONV_CH), main(D_SSM), main(LANES), lay((1, LANES)), lay((1, LANES)),
                  lay((1, D_SSM)), lay((1, D_SSM))],
        out_specs=main(D_SSM),
        out_shape=jax.ShapeDtypeStruct((batch * seq, D_SSM), BF16),
        scratch_shapes=[pltpu.VMEM((SSM_HEADS // 2, SSM_STATE, 2 * SSM_HEAD_DIM), F32)],
        compiler_params=_params(("arbitrary", "arbitrary")),
        name="ssd",
    )(xbc, z, sm, p["dt_bias"], p["a_log"], p["d_skip"], p["ssm_norm"])


def _attn_tile(q_ref, k_ref, v_ref, o_ref, *, c):
    tq = ATTN_SUB
    ri = lax.broadcasted_iota(jnp.int32, (tq, tq), 0)
    ci = lax.broadcasted_iota(jnp.int32, (tq, tq), 1)
    diag_mask = ri >= ci
    lane = lax.broadcasted_iota(jnp.int32, (tq, PV_WIDTH), 1)
    heads_per_pv = PV_WIDTH // V_DIM
    for sub in range(ATTN_STEP_SUBS):
        t = c * ATTN_STEP_SUBS + sub
        kv_len = (t + 1) * tq
        rows = slice(sub * tq, (sub + 1) * tq)
        out = None
        for hh in range(ATTN_HEADS):
            hs = slice(hh * LANES, (hh + 1) * LANES)
            r = hh % heads_per_pv
            vcols = slice((hh - r) * V_DIM, (hh - r) * V_DIM + PV_WIDTH)
            s = _dot_nt(q_ref[rows, hs], k_ref[0:kv_len, hs])
            diag = jnp.where(diag_mask, s[:, t * tq:], -jnp.inf)
            s = diag if t == 0 else jnp.concatenate([s[:, :t * tq], diag], axis=-1)
            m = jnp.max(s, axis=-1, keepdims=True)
            pr = jnp.exp2(s - m)
            l = jnp.sum(pr, axis=-1, keepdims=True)
            o = _dot(pr.astype(BF16), v_ref[0:kv_len, vcols]) / l
            out = o if r == 0 else jnp.where(lane >= r * V_DIM, o, out)
            if r == heads_per_pv - 1:
                o_ref[rows, vcols] = out.astype(BF16)


def _attn_kernel(q_ref, k_ref, v_ref, o_ref, *, nsteps):
    i = pl.program_id(2)
    for c in range(nsteps):
        pl.when(i == c)(functools.partial(_attn_tile, q_ref, k_ref, v_ref, o_ref, c=c))


def _mla_attn(q, k, v, batch, seq):
    tstep = ATTN_SUB * ATTN_STEP_SUBS
    nsteps = seq // tstep
    qw = ATTN_HEADS * LANES
    vw = ATTN_HEADS * V_DIM
    return pl.pallas_call(
        functools.partial(_attn_kernel, nsteps=nsteps),
        grid=(batch, MLA_HEADS // ATTN_HEADS, nsteps),
        in_specs=[pl.BlockSpec((tstep, qw), lambda b, j, i: (b * nsteps + i, j)),
                  pl.BlockSpec((seq, qw), lambda b, j, i: (b, j)),
                  pl.BlockSpec((seq, vw), lambda b, j, i: (b, j))],
        out_specs=pl.BlockSpec((tstep, vw), lambda b, j, i: (b * nsteps + i, j)),
        out_shape=jax.ShapeDtypeStruct((batch * seq, D_ATTN), BF16),
        compiler_params=_params(("parallel", "parallel", "arbitrary")),
        name="mla_attn",
    )(q, k, v)


def _mem_kv_kernel(m_ref, g_ref, wk_ref, wv_ref, k_ref, v_ref):
    h = _rms(m_ref[...], g_ref[...]).astype(BF16)
    k_ref[...] = _dot(h, wk_ref[...]).astype(BF16)
    v_ref[...] = _dot(h, wv_ref[...]).astype(BF16)


def _mem_kv(mem, g, wk, wv):
    depth = g.shape[0]
    batch, ml, _ = mem.shape
    rows = batch * ml
    lay = lambda shape: pl.BlockSpec((None,) + shape, lambda l: (l, 0, 0))
    return pl.pallas_call(
        _mem_kv_kernel,
        grid=(depth,),
        in_specs=[pl.BlockSpec((rows, D_MODEL), lambda l: (0, 0)),
                  lay((1, D_MODEL)), lay((D_MODEL, D_MODEL)), lay((D_MODEL, D_MODEL))],
        out_specs=[lay((rows, D_MODEL)), lay((rows, D_MODEL))],
        out_shape=[jax.ShapeDtypeStruct((depth, rows, D_MODEL), BF16)] * 2,
        compiler_params=_params(("arbitrary",)),
        name="mem_kv",
    )(mem.reshape(rows, D_MODEL), g, wk, wv)


def _mix_out_mem_kernel(x_ref, ys_ref, ya_ref, ag_ref, wo_ref, mg_ref, wq_ref, km_ref, vm_ref,
                        wmo_ref, o_ref):
    ya = _rms(ya_ref[...].astype(F32), ag_ref[...]).astype(BF16)
    x1 = x_ref[...] + _dot(ys_ref[...], wo_ref[0:D_SSM, :]) + _dot(ya, wo_ref[D_SSM:, :])
    hq = _rms(x1, mg_ref[...]).astype(BF16)
    qm = (_dot(hq, wq_ref[...]) * (MEM_HEAD_DIM ** -0.5)).astype(BF16)
    outs = []
    for hd in range(MEM_HEADS):
        hs = slice(hd * MEM_HEAD_DIM, (hd + 1) * MEM_HEAD_DIM)
        s = _dot_nt(qm[:, hs], km_ref[:, hs])
        m = jnp.max(s, axis=-1, keepdims=True)
        pr = jnp.exp(s - m)
        l = jnp.sum(pr, axis=-1, keepdims=True)
        outs.append(_dot(pr.astype(BF16), vm_ref[:, hs]) / l)
    o = jnp.concatenate(outs, axis=-1).astype(BF16)
    o_ref[...] = x1 + _dot(o, wmo_ref[...])


def _mix_out_mem(x2d, ys, ya, kmem, vmem, l, p, batch, seq, tm):
    ns = seq // tm
    ml = kmem.shape[1] // batch
    row = lambda w: pl.BlockSpec((tm, w), lambda b, i: (b * ns + i, 0))
    lay = lambda shape: _const_spec((None,) + shape, (l, 0, 0))
    memspec = pl.BlockSpec((None, ml, D_MODEL), lambda b, i: (l, b, 0))
    return pl.pallas_call(
        _mix_out_mem_kernel,
        grid=(batch, ns),
        in_specs=[row(D_MODEL), row(D_SSM), row(D_ATTN), lay((1, D_ATTN)),
                  lay((D_SSM + D_ATTN, D_MODEL)), lay((1, D_MODEL)), lay((D_MODEL, D_MODEL)),
                  memspec, memspec, lay((D_MODEL, D_MODEL))],
        out_specs=row(D_MODEL),
        out_shape=jax.ShapeDtypeStruct(x2d.shape, F32),
        compiler_params=_params(("parallel", "parallel")),
        name="mix_out_mem",
    )(x2d, ys, ya, p["attn_out_norm"], p["w_out"], p["norm_mem_q"], p["w_mq"], kmem, vmem,
      p["w_mo"])


def _ffn_kernel(x_ref, xh_ref, g_ref, wup_ref, cw_ref, cb_ref, wdn_ref, fg_ref, o_ref, act_ref, *,
                final):
    i = pl.program_id(1)
    x = x_ref[...]
    xh = jnp.where(i > 0, xh_ref[...], 0.0)
    h = _rms(jnp.concatenate([xh, x], axis=0), g_ref[...]).astype(BF16)

    def conv(u, cols):
        w = cw_ref[:, cols]
        out = cb_ref[:, cols] + w[FFN_CONV - 1:FFN_CONV, :] * u[SUBLANES:, :]
        for j in range(FFN_CONV - 1):
            shift = FFN_CONV - 1 - j
            out = out + w[j:j + 1, :] * pltpu.roll(u, shift, 0)[SUBLANES:, :]
        return out

    nchunks = D_FF // FFN_CHUNK
    gcols = lambda c: slice(c * FFN_CHUNK, (c + 1) * FFN_CHUNK)
    vcols = lambda c: slice(D_FF + c * FFN_CHUNK, D_FF + (c + 1) * FFN_CHUNK)
    up = lambda c: (_dot(h, wup_ref[:, gcols(c)]), _dot(h, wup_ref[:, vcols(c)]))

    nxt = up(0)
    for c in range(nchunks):
        ug, uv = nxt
        if c + 1 < nchunks:
            nxt = up(c + 1)
        act_ref[:, gcols(c)] = (_silu(conv(ug, gcols(c))) * conv(uv, vcols(c))).astype(BF16)
    acc = x + _dot(act_ref[...], wdn_ref[...])
    if final:
        acc = _rms(acc, fg_ref[...])
    o_ref[...] = acc


def _ffn(x2d, l, p, final_norm, batch, seq, tm, final):
    ns = seq // tm
    halo_blocks_per_seq = seq // SUBLANES
    row = pl.BlockSpec((tm, D_MODEL), lambda b, i: (b * ns + i, 0))
    halo = pl.BlockSpec(
        (SUBLANES, D_MODEL),
        lambda b, i: (jnp.maximum(b * halo_blocks_per_seq + i * (tm // SUBLANES) - 1, 0), 0))
    lay = lambda shape: _const_spec((None,) + shape, (l, 0, 0))
    return pl.pallas_call(
        functools.partial(_ffn_kernel, final=final),
        grid=(batch, ns),
        in_specs=[row, halo, lay((1, D_MODEL)), lay((D_MODEL, 2 * D_FF)), lay((FFN_CONV, 2 * D_FF)),
                  lay((1, 2 * D_FF)), lay((D_FF, D_MODEL)), _const_spec((1, D_MODEL), (0, 0))],
        out_specs=row,
        out_shape=jax.ShapeDtypeStruct(x2d.shape, F32),
        compiler_params=_params(("parallel", "parallel")),
        scratch_shapes=[pltpu.VMEM((tm, D_FF), BF16)],
        name="ffn_final" if final else "ffn",
    )(x2d, x2d, p["norm_ffn"], p["w_up"], p["ffn_conv_w"], p["ffn_conv_b"], p["w_down"], final_norm)


def _prep_params(norm_mix, w_in, ssm_conv_w, ssm_conv_b, dt_bias, a_log, d_skip, ssm_norm, q_norm,
                 w_uq, kv_norm, w_ukv, attn_out_norm, w_out, norm_mem_q, w_mq, w_mo, norm_ffn, w_up,
                 ffn_conv_w, ffn_conv_b, w_down):
    depth = w_in.shape[0]
    o1 = D_SSM
    o2 = o1 + CONV_CH
    o3 = o2 + SSM_HEADS
    o4 = o3 + Q_LORA
    o5 = o4 + KV_LORA
    vec = lambda a: a.reshape(depth, 1, -1)
    w_main = w_in.astype(BF16)
    zeros = lambda w: jnp.zeros((depth, D_MODEL, w), w_in.dtype)
    w_lat = jnp.concatenate([w_in[..., o3:o5], zeros(ROPE_LO), w_in[..., o5:], w_in[..., o2:o3],
                             zeros(LANES - DT_HI)], axis=-1).astype(BF16)
    uq = w_uq.reshape(depth, Q_LORA, MLA_HEADS, QK_NOPE + QK_ROPE)
    uq = jnp.pad(uq, ((0, 0), (0, 0), (0, 0), (0, LANES - QK_NOPE - QK_ROPE)))
    ukv = w_ukv.reshape(depth, KV_LORA, MLA_HEADS, QK_NOPE + V_DIM)
    uk = jnp.pad(ukv[..., :QK_NOPE], ((0, 0), (0, 0), (0, 0), (0, LANES - QK_NOPE)))
    uv = ukv[..., QK_NOPE:]
    lane_pad = lambda a: jnp.pad(a, ((0, 0), (DT_LO, LANES - DT_HI))).reshape(depth, 1, LANES)
    return {
        "norm_mix": vec(norm_mix), "w_main": w_main, "w_lat": w_lat,
        "q_norm": vec(q_norm), "w_uq": uq.reshape(depth, Q_LORA, MLA_HEADS * LANES).astype(BF16),
        "kv_norm": vec(kv_norm), "w_uk": uk.reshape(depth, KV_LORA, MLA_HEADS * LANES).astype(BF16),
        "w_uv": uv.reshape(depth, KV_LORA, D_ATTN).astype(BF16),
        "ssm_conv_w": ssm_conv_w, "ssm_conv_b": vec(ssm_conv_b),
        "dt_bias": lane_pad(dt_bias), "a_log": lane_pad(a_log),
        "d_skip": jnp.repeat(d_skip, SSM_HEAD_DIM, axis=-1).reshape(depth, 1, D_SSM),
        "ssm_norm": vec(ssm_norm), "attn_out_norm": vec(attn_out_norm),
        "w_out": w_out.astype(BF16), "norm_mem_q": vec(norm_mem_q), "w_mq": w_mq.astype(BF16),
        "w_mo": w_mo.astype(BF16), "norm_ffn": vec(norm_ffn), "w_up": w_up.astype(BF16),
        "ffn_conv_w": ffn_conv_w, "ffn_conv_b": vec(ffn_conv_b), "w_down": w_down.astype(BF16),
    }


def kernel(x, mem, positions, norm_mix, w_in, ssm_conv_w, ssm_conv_b, dt_bias, a_log, d_skip,
           ssm_norm, q_norm, w_uq, kv_norm, w_ukv, attn_out_norm, w_out, norm_mem_q, norm_mem_kv,
           w_mq, w_mk, w_mv, w_mo, norm_ffn, w_up, ffn_conv_w, ffn_conv_b, w_down, final_norm):
    batch, seq, _ = x.shape
    depth = w_in.shape[0]
    p = _prep_params(norm_mix, w_in, ssm_conv_w, ssm_conv_b, dt_bias, a_log, d_skip, ssm_norm,
                     q_norm, w_uq, kv_norm, w_ukv, attn_out_norm, w_out, norm_mem_q, w_mq, w_mo,
                     norm_ffn, w_up, ffn_conv_w, ffn_conv_b, w_down)
    ctab, stab = _rope_tables(positions)
    kmem, vmem = _mem_kv(mem, norm_mem_kv.reshape(depth, 1, D_MODEL), w_mk.astype(BF16),
                         w_mv.astype(BF16))
    fnorm = final_norm.reshape(1, D_MODEL)
    tm_in = min(512, seq)
    tm = min(1024, seq)
    tc = min(512, seq)
    x2d = x.reshape(batch * seq, D_MODEL)
    for l in range(depth):
        z, xbc, sm, q, k, v = _mix_in(x2d, l, p, ctab, stab, seq, tm_in)
        ys = _ssd(xbc, z, sm, l, p, batch, seq, tc)
        ya = _mla_attn(q, k, v, batch, seq)
        x2d = _mix_out_mem(x2d, ys, ya, kmem, vmem, l, p, batch, seq, tm)
        x2d = _ffn(x2d, l, p, fnorm, batch, seq, tm, final=(l == depth - 1))
    return x2d.reshape(batch, seq, D_MODEL)
```

```python
import functools
import math

import jax
import jax.numpy as jnp
from jax import lax
from jax.experimental import pallas as pl
from jax.experimental.pallas import tpu as pltpu

F32 = jnp.float32
BF16 = jnp.bfloat16

D_MODEL = 1024
EPS = 1e-6
SSM_HEADS = 16
SSM_HEAD_DIM = 64
D_SSM = SSM_HEADS * SSM_HEAD_DIM
SSM_GROUPS = 4
SSM_STATE = 128
SSM_CONV = 4
SSM_CHUNK = 128
CONV_CH = D_SSM + 2 * SSM_GROUPS * SSM_STATE
MLA_HEADS = 16
QK_NOPE = 64
QK_ROPE = 32
V_DIM = 64
Q_LORA = 384
KV_LORA = 256
D_ATTN = MLA_HEADS * V_DIM
ROPE_THETA = 10000.0
MEM_HEADS = 4
MEM_HEAD_DIM = D_MODEL // MEM_HEADS
D_FF = 2816
FFN_CONV = 3

LANES = 128
SUBLANES = 8
MXU_TILE = 256
VMEM_LIMIT = 56 * 1024 * 1024

ROPE_LO = QK_NOPE
ROPE_MID = QK_NOPE + QK_ROPE // 2
ROPE_HI = QK_NOPE + QK_ROPE
DT_LO = ROPE_HI
DT_HI = ROPE_HI + SSM_HEADS

FFN_CHUNK = 256
CONV_CHUNK = 256
assert SSM_CONV == 4 and CONV_CH % CONV_CHUNK == 0
ATTN_SUB = 256
ATTN_STEP_SUBS = 2
ATTN_HEADS = 8
PV_WIDTH = 256


def _dot(a, b):
    return jnp.dot(a, b, preferred_element_type=F32)


def _dot_nt(a, b):
    return lax.dot_general(a, b, (((1,), (1,)), ((), ())), preferred_element_type=F32)


def _rms(xf, g):
    var = jnp.mean(xf * xf, axis=-1, keepdims=True)
    return xf * lax.rsqrt(var + EPS) * g


def _silu(x):
    h = 0.5 * x
    return h + h * jnp.tanh(h)


def _softplus(x):
    return jnp.maximum(x, 0.0) + jnp.log1p(jnp.exp(-jnp.abs(x)))


def _rope128(t, c, s, lane):
    rot = jnp.where(lane < ROPE_MID, pltpu.roll(t, LANES - QK_ROPE // 2, 1),
                    pltpu.roll(t, QK_ROPE // 2, 1))
    return t * c + rot * s


def _params(sem):
    return pltpu.CompilerParams(dimension_semantics=sem, vmem_limit_bytes=VMEM_LIMIT)


def _const_spec(shape, index):
    return pl.BlockSpec(shape, lambda *_: index, pipeline_mode=pl.Buffered(1))


def _rope_kernel(pos_ref, inv_ref, cos_ref, sin_ref):
    ang = pos_ref[...] * inv_ref[...]
    cos_ref[...] = jnp.cos(ang)
    sin_ref[...] = jnp.sin(ang)


def _rope_tables(positions):
    t = positions.size
    half = QK_ROPE // 2
    per_row = LANES // half
    inv_freq = 1.0 / (ROPE_THETA ** (jnp.arange(0, QK_ROPE, 2, dtype=F32) / QK_ROPE))
    posd = jnp.repeat(positions.reshape(t).astype(F32), half).reshape(t // per_row, LANES)
    invd = jnp.tile(inv_freq, per_row).reshape(1, LANES)
    rows = t // per_row
    cosd, sind = pl.pallas_call(
        _rope_kernel,
        out_shape=(jax.ShapeDtypeStruct((rows, LANES), F32),) * 2,
        name="rope_tables",
    )(posd, invd)
    cos = cosd.reshape(t, half)
    sin = sind.reshape(t, half)
    ctab = jnp.concatenate([jnp.ones((t, QK_NOPE), F32), cos, cos,
                            jnp.zeros((t, LANES - ROPE_HI), F32)], axis=-1)
    stab = jnp.concatenate([jnp.zeros((t, QK_NOPE), F32), -sin, sin,
                            jnp.zeros((t, LANES - ROPE_HI), F32)], axis=-1)
    return ctab, stab


_O_Z = 0
_O_XBC = D_SSM
_W_MAIN = _O_XBC + CONV_CH
_L_CQ = 0
_L_CKV = Q_LORA
_L_SM = Q_LORA + KV_LORA
_W_LAT = _L_SM + LANES
_QK_SCALE = (QK_NOPE + QK_ROPE) ** -0.5 * math.log2(math.e)


def _mix_in_kernel(x_ref, xh_ref, g_ref, wm_ref, wl_ref, cw_ref, cb_ref, qg_ref, wuq_ref, kg_ref,
                   wuk_ref, wuv_ref, c_ref, s_ref, z_ref, xbc_ref, sm_ref, q_ref, k_ref, v_ref,
                   *, tiles_per_seq):
    first = pl.program_id(0) % tiles_per_seq == 0
    xh = jnp.where(first, 0.0, xh_ref[...])
    h_full = _rms(jnp.concatenate([xh, x_ref[...]], axis=0), g_ref[...]).astype(BF16)
    h = h_full[SUBLANES:, :]

    def xbc_dot(cc):
        lo = _O_XBC + cc * CONV_CHUNK
        return _dot(h_full, wm_ref[:, lo:lo + CONV_CHUNK])

    def conv_store(cc, u):
        cols = slice(cc * CONV_CHUNK, (cc + 1) * CONV_CHUNK)
        cw = cw_ref[:, cols]
        u1 = pltpu.roll(u, 1, 0)
        near = cw[3:4, :] * u + cw[2:3, :] * u1
        far = cw[1:2, :] * u + cw[0:1, :] * u1
        acc = cb_ref[:, cols] + near[SUBLANES:, :] + pltpu.roll(far, 2, 0)[SUBLANES:, :]
        xbc_ref[:, cols] = _silu(acc).astype(BF16)

    w = MXU_TILE
    heads_per_tile = w // LANES

    u = xbc_dot(0)
    cq = _dot(h, wl_ref[:, _L_CQ:_L_CKV])
    ckv = _dot(h, wl_ref[:, _L_CKV:_L_SM])
    sm = _dot(h, wl_ref[:, _L_SM:_W_LAT])
    sm_ref[...] = sm
    c = c_ref[...]
    s = s_ref[...]
    lane = lax.broadcasted_iota(jnp.int32, c.shape, 1)
    cs = c * _QK_SCALE
    ss = s * _QK_SCALE
    cqn = _rms(cq, qg_ref[...]).astype(BF16)
    ckvn = _rms(ckv, kg_ref[...]).astype(BF16)
    kpe = _rope128(sm, c, s, lane)
    kpe = jnp.where((lane >= ROPE_LO) & (lane < ROPE_HI), kpe, 0.0)

    def z_tile(j):
        z_ref[:, j * w:(j + 1) * w] = _dot(h, wm_ref[:, _O_Z + j * w:_O_Z + (j + 1) * w]).astype(BF16)

    def q_tile(j):
        qq = _dot(cqn, wuq_ref[:, j * w:(j + 1) * w])
        for r in range(heads_per_tile):
            sl = slice(j * w + r * LANES, j * w + (r + 1) * LANES)
            q_ref[:, sl] = _rope128(qq[:, r * LANES:(r + 1) * LANES], cs, ss, lane).astype(BF16)

    def k_tile(j):
        kk = _dot(ckvn, wuk_ref[:, j * w:(j + 1) * w])
        for r in range(heads_per_tile):
            sl = slice(j * w + r * LANES, j * w + (r + 1) * LANES)
            k_ref[sl, :] = (kk[:, r * LANES:(r + 1) * LANES] + kpe).T.astype(BF16)

    def v_tile(j):
        v_ref[:, j * w:(j + 1) * w] = _dot(ckvn, wuv_ref[:, j * w:(j + 1) * w]).astype(BF16)

    nqk = MLA_HEADS // heads_per_tile
    tasks = ([functools.partial(z_tile, j) for j in range(D_SSM // w)]
             + [functools.partial(q_tile, j) for j in range(nqk)]
             + [functools.partial(k_tile, j) for j in range(nqk)]
             + [functools.partial(v_tile, j) for j in range(D_ATTN // w)])
    nconv = CONV_CH // CONV_CHUNK
    per = -(-len(tasks) // nconv)
    for cc in range(nconv):
        nxt = xbc_dot(cc + 1) if cc + 1 < nconv else None
        conv_store(cc, u)
        for task in tasks[cc * per:(cc + 1) * per]:
            task()
        u = nxt


def _mix_in(x2d, l, p, ctab, stab, seq, tm):
    t = x2d.shape[0]
    hp = MLA_HEADS * LANES
    row = lambda w: pl.BlockSpec((tm, w), lambda i: (i, 0))
    halo = pl.BlockSpec((SUBLANES, D_MODEL),
                        lambda i: (jnp.maximum(i * (tm // SUBLANES) - 1, 0), 0))
    lay = lambda shape: _const_spec((None,) + shape, (l, 0, 0))
    return pl.pallas_call(
        functools.partial(_mix_in_kernel, tiles_per_seq=seq // tm),
        grid=(t // tm,),
        in_specs=[row(D_MODEL), halo, lay((1, D_MODEL)), lay((D_MODEL, _W_MAIN)),
                  lay((D_MODEL, _W_LAT)), lay((SSM_CONV, CONV_CH)), lay((1, CONV_CH)),
                  lay((1, Q_LORA)), lay((Q_LORA, hp)), lay((1, KV_LORA)), lay((KV_LORA, hp)),
                  lay((KV_LORA, D_ATTN)), row(LANES), row(LANES)],
        out_specs=[row(D_SSM), row(CONV_CH), row(LANES), row(hp),
                   pl.BlockSpec((hp, tm), lambda i: (0, i)), row(D_ATTN)],
        out_shape=[jax.ShapeDtypeStruct((t, D_SSM), BF16), jax.ShapeDtypeStruct((t, CONV_CH), BF16),
                   jax.ShapeDtypeStruct((t, LANES), F32), jax.ShapeDtypeStruct((t, hp), BF16),
                   jax.ShapeDtypeStruct((hp, t), BF16), jax.ShapeDtypeStruct((t, D_ATTN), BF16)],
        compiler_params=_params(("parallel",)),
        name="mix_in",
    )(x2d, x2d, p["norm_mix"], p["w_main"], p["w_lat"], p["ssm_conv_w"], p["ssm_conv_b"],
      p["q_norm"], p["w_uq"], p["kv_norm"], p["w_uk"], p["w_uv"], ctab, stab)


def _split3(a):
    hi = a.astype(BF16)
    r = a - hi.astype(F32)
    mid = r.astype(BF16)
    lo = (r - mid.astype(F32)).astype(BF16)
    return hi, mid, lo


def _ssd_kernel(xbc_ref, z_ref, sm_ref, dtb_ref, alog_ref, dsk_ref, g_ref, o_ref, state_ref, *, tc):
    i = pl.program_id(1)

    @pl.when(i == 0)
    def _():
        state_ref[...] = jnp.zeros_like(state_ref)

    xc = xbc_ref[...].astype(F32)

    n = SSM_CHUNK
    ri = lax.broadcasted_iota(jnp.int32, (n, n), 0)
    ci = lax.broadcasted_iota(jnp.int32, (n, n), 1)
    causal = ri >= ci
    tri = jnp.where(causal, 1.0, 0.0).astype(BF16)
    lane = lax.broadcasted_iota(jnp.int32, (1, LANES), 1)
    dt_lanes = (lane >= DT_LO) & (lane < DT_HI)
    a_tile = jnp.where(dt_lanes, -jnp.exp(alog_ref[...]), 0.0)
    dtb = dtb_ref[...]
    bo = D_SSM
    co = D_SSM + SSM_GROUPS * SSM_STATE
    heads_per_group = SSM_HEADS // SSM_GROUPS
    lo_half = lax.broadcasted_iota(jnp.int32, (n, LANES), 1) < SSM_HEAD_DIM
    lo_row = lax.broadcasted_iota(jnp.int32, (1, LANES), 1) < SSM_HEAD_DIM

    for c in range(tc // n):
        rows = slice(c * n, (c + 1) * n)
        xs = xc[rows, 0:D_SSM]
        dtf = _softplus(sm_ref[rows, :] + dtb)
        ad = dtf * a_tile
        hi, mid, lo = _split3(ad)
        cs = _dot(tri, hi) + _dot(tri, mid) + _dot(tri, lo)
        cs_t = cs.T
        dt_t = dtf.T
        ys = []
        for g in range(SSM_GROUPS):
            bg = xc[rows, bo + g * SSM_STATE: bo + (g + 1) * SSM_STATE]
            cg = xc[rows, co + g * SSM_STATE: co + (g + 1) * SSM_STATE]
            bg_t = bg.T
            cg16 = cg.astype(BF16)
            cb = _dot_nt(cg16, bg.astype(BF16))
            for q in range(heads_per_group // 2):
                pair = g * (heads_per_group // 2) + q
                xp = xs[:, pair * LANES:(pair + 1) * LANES]
                halves = (jnp.where(lo_half, xp, 0.0).astype(BF16),
                          jnp.where(lo_half, 0.0, xp).astype(BF16))
                prev = state_ref[pair]
                y_diag = st_new = None
                e_cols, e_last = [], []
                for k in range(2):
                    ln = DT_LO + 2 * pair + k
                    col = cs[:, ln:ln + 1]
                    row = cs_t[ln:ln + 1, :]
                    dt_row = dt_t[ln:ln + 1, :]
                    last = cs[n - 1:n, ln:ln + 1]
                    lmat = jnp.exp(jnp.where(causal, col - row, -jnp.inf))
                    mm = (cb * lmat * dt_row).astype(BF16)
                    w_row = jnp.exp(last - row) * dt_row
                    yd = _dot(mm, halves[k])
                    sn = _dot((bg_t * w_row).astype(BF16), halves[k])
                    y_diag = yd if k == 0 else y_diag + yd
                    st_new = sn if k == 0 else st_new + sn
                    e_cols.append(jnp.exp(col))
                    e_last.append(jnp.exp(last))
                y_off = _dot(cg16, prev.astype(BF16))
                ys.append(y_diag + jnp.where(lo_half, e_cols[0], e_cols[1]) * y_off)
                state_ref[pair] = prev * jnp.where(lo_row, e_last[0], e_last[1]) + st_new
        y = jnp.concatenate(ys, axis=-1) + xs * dsk_ref[...]
        y = y * _silu(z_ref[rows, :].astype(F32))
        o_ref[rows, :] = _rms(y, g_ref[...]).astype(BF16)


def _ssd(xbc, z, sm, l, p, batch, seq, tc):
    ns = seq // tc
    main = lambda w: pl.BlockSpec((tc, w), lambda b, i: (b * ns + i, 0))
    lay = lambda shape: _const_spec((None,) + shape, (l, 0, 0))
    return pl.pallas_call(
        functools.partial(_ssd_kernel, tc=tc),
        grid=(batch, ns),
        in_specs=[main(CONV_CH), main(D_SSM), main(LANES), lay((1, LANES)), lay((1, LANES)),
                  lay((1, D_SSM)), lay((1, D_SSM))],
        out_specs=main(D_SSM),
        out_shape=jax.ShapeDtypeStruct((batch * seq, D_SSM), BF16),
        scratch_shapes=[pltpu.VMEM((SSM_HEADS // 2, SSM_STATE, 2 * SSM_HEAD_DIM), F32)],
        compiler_params=_params(("arbitrary", "arbitrary")),
        name="ssd",
    )(xbc, z, sm, p["dt_bias"], p["a_log"], p["d_skip"], p["ssm_norm"])


def _attn_tile(q_ref, k_ref, v_ref, o_ref, *, c):
    tq = ATTN_SUB
    ri = lax.broadcasted_iota(jnp.int32, (tq, tq), 0)
    ci = lax.broadcasted_iota(jnp.int32, (tq, tq), 1)
    diag_mask = ri >= ci
    lane = lax.broadcasted_iota(jnp.int32, (tq, PV_WIDTH), 1)
    heads_per_pv = PV_WIDTH // V_DIM
    for sub in range(ATTN_STEP_SUBS):
        t = c * ATTN_STEP_SUBS + sub
        kv_len = (t + 1) * tq
        rows = slice(sub * tq, (sub + 1) * tq)
        out = None
        for hh in range(ATTN_HEADS):
            hs = slice(hh * LANES, (hh + 1) * LANES)
            r = hh % heads_per_pv
            vcols = slice((hh - r) * V_DIM, (hh - r) * V_DIM + PV_WIDTH)
            s = _dot(q_ref[rows, hs], k_ref[hs, 0:kv_len])
            diag = jnp.where(diag_mask, s[:, t * tq:], -jnp.inf)
            s = diag if t == 0 else jnp.concatenate([s[:, :t * tq], diag], axis=-1)
            m = jnp.max(s, axis=-1, keepdims=True)
            pr = jnp.exp2(s - m)
            l = jnp.sum(pr, axis=-1, keepdims=True)
            o = _dot(pr.astype(BF16), v_ref[0:kv_len, vcols]) / l
            out = o if r == 0 else jnp.where(lane >= r * V_DIM, o, out)
            if r == heads_per_pv - 1:
                o_ref[rows, vcols] = out.astype(BF16)


def _attn_kernel(q_ref, k_ref, v_ref, o_ref, *, nsteps):
    i = pl.program_id(2)
    for c in range(nsteps):
        pl.when(i == c)(functools.partial(_attn_tile, q_ref, k_ref, v_ref, o_ref, c=c))


def _mla_attn(q, k, v, batch, seq):
    tstep = ATTN_SUB * ATTN_STEP_SUBS
    nsteps = seq // tstep
    qw = ATTN_HEADS * LANES
    vw = ATTN_HEADS * V_DIM
    return pl.pallas_call(
        functools.partial(_attn_kernel, nsteps=nsteps),
        grid=(batch, MLA_HEADS // ATTN_HEADS, nsteps),
        in_specs=[pl.BlockSpec((tstep, qw), lambda b, j, i: (b * nsteps + i, j)),
                  pl.BlockSpec((qw, seq), lambda b, j, i: (j, b)),
                  pl.BlockSpec((seq, vw), lambda b, j, i: (b, j))],
        out_specs=pl.BlockSpec((tstep, vw), lambda b, j, i: (b * nsteps + i, j)),
        out_shape=jax.ShapeDtypeStruct((batch * seq, D_ATTN), BF16),
        compiler_params=_params(("parallel", "parallel", "arbitrary")),
        name="mla_attn",
    )(q, k, v)


def _mem_kv_kernel(m_ref, g_ref, wk_ref, wv_ref, k_ref, v_ref):
    h = _rms(m_ref[...], g_ref[...]).astype(BF16)
    k_ref[...] = _dot(h, wk_ref[...]).astype(BF16)
    v_ref[...] = _dot(h, wv_ref[...]).astype(BF16)


def _mem_kv(mem, g, wk, wv):
    depth = g.shape[0]
    batch, ml, _ = mem.shape
    rows = batch * ml
    lay = lambda shape: pl.BlockSpec((None,) + shape, lambda l: (l, 0, 0))
    return pl.pallas_call(
        _mem_kv_kernel,
        grid=(depth,),
        in_specs=[pl.BlockSpec((rows, D_MODEL), lambda l: (0, 0)),
                  lay((1, D_MODEL)), lay((D_MODEL, D_MODEL)), lay((D_MODEL, D_MODEL))],
        out_specs=[lay((rows, D_MODEL)), lay((rows, D_MODEL))],
        out_shape=[jax.ShapeDtypeStruct((depth, rows, D_MODEL), BF16)] * 2,
        compiler_params=_params(("arbitrary",)),
        name="mem_kv",
    )(mem.reshape(rows, D_MODEL), g, wk, wv)


def _mix_out_mem_kernel(x_ref, ys_ref, ya_ref, ag_ref, wo_ref, mg_ref, wq_ref, km_ref, vm_ref,
                        wmo_ref, o_ref):
    ya = _rms(ya_ref[...].astype(F32), ag_ref[...]).astype(BF16)
    x1 = x_ref[...] + _dot(ys_ref[...], wo_ref[0:D_SSM, :]) + _dot(ya, wo_ref[D_SSM:, :])
    hq = _rms(x1, mg_ref[...]).astype(BF16)
    qm = (_dot(hq, wq_ref[...]) * (MEM_HEAD_DIM ** -0.5)).astype(BF16)
    outs = []
    for hd in range(MEM_HEADS):
        hs = slice(hd * MEM_HEAD_DIM, (hd + 1) * MEM_HEAD_DIM)
        s = _dot_nt(qm[:, hs], km_ref[:, hs])
        m = jnp.max(s, axis=-1, keepdims=True)
        pr = jnp.exp(s - m)
        l = jnp.sum(pr, axis=-1, keepdims=True)
        outs.append(_dot(pr.astype(BF16), vm_ref[:, hs]) / l)
    o = jnp.concatenate(outs, axis=-1).astype(BF16)
    o_ref[...] = x1 + _dot(o, wmo_ref[...])


def _mix_out_mem(x2d, ys, ya, kmem, vmem, l, p, batch, seq, tm):
    ns = seq // tm
    ml = kmem.shape[1] // batch
    row = lambda w: pl.BlockSpec((tm, w), lambda b, i: (b * ns + i, 0))
    lay = lambda shape: _const_spec((None,) + shape, (l, 0, 0))
    memspec = pl.BlockSpec((None, ml, D_MODEL), lambda b, i: (l, b, 0))
    return pl.pallas_call(
        _mix_out_mem_kernel,
        grid=(batch, ns),
        in_specs=[row(D_MODEL), row(D_SSM), row(D_ATTN), lay((1, D_ATTN)),
                  lay((D_SSM + D_ATTN, D_MODEL)), lay((1, D_MODEL)), lay((D_MODEL, D_MODEL)),
                  memspec, memspec, lay((D_MODEL, D_MODEL))],
        out_specs=row(D_MODEL),
        out_shape=jax.ShapeDtypeStruct(x2d.shape, F32),
        compiler_params=_params(("parallel", "parallel")),
        name="mix_out_mem",
    )(x2d, ys, ya, p["attn_out_norm"], p["w_out"], p["norm_mem_q"], p["w_mq"], kmem, vmem,
      p["w_mo"])


def _ffn_kernel(x_ref, xh_ref, g_ref, wup_ref, cw_ref, cb_ref, wdn_ref, fg_ref, o_ref, act_ref, *,
                final):
    i = pl.program_id(1)
    x = x_ref[...]
    xh = jnp.where(i > 0, xh_ref[...], 0.0)
    h = _rms(jnp.concatenate([xh, x], axis=0), g_ref[...]).astype(BF16)

    def conv(u, cols):
        w = cw_ref[:, cols]
        out = cb_ref[:, cols] + w[FFN_CONV - 1:FFN_CONV, :] * u[SUBLANES:, :]
        for j in range(FFN_CONV - 1):
            shift = FFN_CONV - 1 - j
            out = out + w[j:j + 1, :] * pltpu.roll(u, shift, 0)[SUBLANES:, :]
        return out

    nchunks = D_FF // FFN_CHUNK
    gcols = lambda c: slice(c * FFN_CHUNK, (c + 1) * FFN_CHUNK)
    vcols = lambda c: slice(D_FF + c * FFN_CHUNK, D_FF + (c + 1) * FFN_CHUNK)
    up = lambda c: (_dot(h, wup_ref[:, gcols(c)]), _dot(h, wup_ref[:, vcols(c)]))

    nxt = up(0)
    for c in range(nchunks):
        ug, uv = nxt
        if c + 1 < nchunks:
            nxt = up(c + 1)
        act_ref[:, gcols(c)] = (_silu(conv(ug, gcols(c))) * conv(uv, vcols(c))).astype(BF16)
    acc = x + _dot(act_ref[...], wdn_ref[...])
    if final:
        acc = _rms(acc, fg_ref[...])
    o_ref[...] = acc


def _ffn(x2d, l, p, final_norm, batch, seq, tm, final):
    ns = seq // tm
    halo_blocks_per_seq = seq // SUBLANES
    row = pl.BlockSpec((tm, D_MODEL), lambda b, i: (b * ns + i, 0))
    halo = pl.BlockSpec(
        (SUBLANES, D_MODEL),
        lambda b, i: (jnp.maximum(b * halo_blocks_per_seq + i * (tm // SUBLANES) - 1, 0), 0))
    lay = lambda shape: _const_spec((None,) + shape, (l, 0, 0))
    return pl.pallas_call(
        functools.partial(_ffn_kernel, final=final),
        grid=(batch, ns),
        in_specs=[row, halo, lay((1, D_MODEL)), lay((D_MODEL, 2 * D_FF)), lay((FFN_CONV, 2 * D_FF)),
                  lay((1, 2 * D_FF)), lay((D_FF, D_MODEL)), _const_spec((1, D_MODEL), (0, 0))],
        out_specs=row,
        out_shape=jax.ShapeDtypeStruct(x2d.shape, F32),
        compiler_params=_params(("parallel", "parallel")),
        scratch_shapes=[pltpu.VMEM((tm, D_FF), BF16)],
        name="ffn_final" if final else "ffn",
    )(x2d, x2d, p["norm_ffn"], p["w_up"], p["ffn_conv_w"], p["ffn_conv_b"], p["w_down"], final_norm)


def _prep_params(norm_mix, w_in, ssm_conv_w, ssm_conv_b, dt_bias, a_log, d_skip, ssm_norm, q_norm,
                 w_uq, kv_norm, w_ukv, attn_out_norm, w_out, norm_mem_q, w_mq, w_mo, norm_ffn, w_up,
                 ffn_conv_w, ffn_conv_b, w_down):
    depth = w_in.shape[0]
    o1 = D_SSM
    o2 = o1 + CONV_CH
    o3 = o2 + SSM_HEADS
    o4 = o3 + Q_LORA
    o5 = o4 + KV_LORA
    vec = lambda a: a.reshape(depth, 1, -1)
    w_main = w_in.astype(BF16)
    zeros = lambda w: jnp.zeros((depth, D_MODEL, w), w_in.dtype)
    w_lat = jnp.concatenate([w_in[..., o3:o5], zeros(ROPE_LO), w_in[..., o5:], w_in[..., o2:o3],
                             zeros(LANES - DT_HI)], axis=-1).astype(BF16)
    uq = w_uq.reshape(depth, Q_LORA, MLA_HEADS, QK_NOPE + QK_ROPE)
    uq = jnp.pad(uq, ((0, 0), (0, 0), (0, 0), (0, LANES - QK_NOPE - QK_ROPE)))
    ukv = w_ukv.reshape(depth, KV_LORA, MLA_HEADS, QK_NOPE + V_DIM)
    uk = jnp.pad(ukv[..., :QK_NOPE], ((0, 0), (0, 0), (0, 0), (0, LANES - QK_NOPE)))
    uv = ukv[..., QK_NOPE:]
    lane_pad = lambda a: jnp.pad(a, ((0, 0), (DT_LO, LANES - DT_HI))).reshape(depth, 1, LANES)
    return {
        "norm_mix": vec(norm_mix), "w_main": w_main, "w_lat": w_lat,
        "q_norm": vec(q_norm), "w_uq": uq.reshape(depth, Q_LORA, MLA_HEADS * LANES).astype(BF16),
        "kv_norm": vec(kv_norm), "w_uk": uk.reshape(depth, KV_LORA, MLA_HEADS * LANES).astype(BF16),
        "w_uv": uv.reshape(depth, KV_LORA, D_ATTN).astype(BF16),
        "ssm_conv_w": ssm_conv_w, "ssm_conv_b": vec(ssm_conv_b),
        "dt_bias": lane_pad(dt_bias), "a_log": lane_pad(a_log),
        "d_skip": jnp.repeat(d_skip, SSM_HEAD_DIM, axis=-1).reshape(depth, 1, D_SSM),
        "ssm_norm": vec(ssm_norm), "attn_out_norm": vec(attn_out_norm),
        "w_out": w_out.astype(BF16), "norm_mem_q": vec(norm_mem_q), "w_mq": w_mq.astype(BF16),
        "w_mo": w_mo.astype(BF16), "norm_ffn": vec(norm_ffn), "w_up": w_up.astype(BF16),
        "ffn_conv_w": ffn_conv_w, "ffn_conv_b": vec(ffn_conv_b), "w_down": w_down.astype(BF16),
    }


def kernel(x, mem, positions, norm_mix, w_in, ssm_conv_w, ssm_conv_b, dt_bias, a_log, d_skip,
           ssm_norm, q_norm, w_uq, kv_norm, w_ukv, attn_out_norm, w_out, norm_mem_q, norm_mem_kv,
           w_mq, w_mk, w_mv, w_mo, norm_ffn, w_up, ffn_conv_w, ffn_conv_b, w_down, final_norm):
    batch, seq, _ = x.shape
    depth = w_in.shape[0]
    p = _prep_params(norm_mix, w_in, ssm_conv_w, ssm_conv_b, dt_bias, a_log, d_skip, ssm_norm,
                     q_norm, w_uq, kv_norm, w_ukv, attn_out_norm, w_out, norm_mem_q, w_mq, w_mo,
                     norm_ffn, w_up, ffn_conv_w, ffn_conv_b, w_down)
    ctab, stab = _rope_tables(positions)
    kmem, vmem = _mem_kv(mem, norm_mem_kv.reshape(depth, 1, D_MODEL), w_mk.astype(BF16),
                         w_mv.astype(BF16))
    fnorm = final_norm.reshape(1, D_MODEL)
    tm_in = min(512, seq)
    tm = min(1024, seq)
    tc = min(512, seq)
    x2d = x.reshape(batch * seq, D_MODEL)
    for l in range(depth):
        z, xbc, sm, q, k, v = _mix_in(x2d, l, p, ctab, stab, seq, tm_in)
        ys = _ssd(xbc, z, sm, l, p, batch, seq, tc)
        ya = _mla_attn(q, k, v, batch, seq)
        x2d = _mix_out_mem(x2d, ys, ya, kmem, vmem, l, p, batch, seq, tm)
        x2d = _ffn(x2d, l, p, fnorm, batch, seq, tm, final=(l == depth - 1))
    return x2d.reshape(batch, seq, D_MODEL)
```
